```python
import math
import jax, jax.numpy as jnp
from jax import lax
import numpy as np

D_MODEL = 1024
BATCH = 16
SEQ = 256
DEPTH = 4
DEC_BATCH = 4
DEC_SEQ = 2048
PAST_LEN = 256

GRID_W = 64
N_HEADS = 16
N_KV_HEADS = 4
HEAD_DIM = D_MODEL // N_HEADS
AXIS_DIM = HEAD_DIM // 2
ROPE_THETA = 10000.0
Q_BLOCK = 128
N_ATTN_LAYERS = (DEPTH + 1) // 2
N_CONV_LAYERS = DEPTH // 2
CONV_W = 3
PEER_HEADS = 8
N_KEYS = 128
N_EXPERTS = N_KEYS * N_KEYS
D_KEY = 256
D_KEY_HALF = D_KEY // 2
PEER_TOPK = 16
TOK_BLOCK = 128
N_MOD = 6
EPS = 1e-6

kernel_name = "hybrid_dit_gqa_shortconv_peer_step"


def rmsnorm(x, g):
    xf = x.astype(jnp.float32)
    y = xf * lax.rsqrt(jnp.mean(xf * xf, axis=-1, keepdims=True) + EPS)
    return (y * g.astype(jnp.float32)).astype(x.dtype)


def modulation(cond, w_ada, b_ada):
    m = jax.nn.silu(cond) @ w_ada + b_ada
    return [t[:, None, :] for t in jnp.split(m, N_MOD, axis=-1)]


def modulate(h, shift, scale):
    return h * (1 + scale) + shift


def axial_angles(n_tokens):
    n_rows = n_tokens // GRID_W
    row = jnp.repeat(jnp.arange(n_rows, dtype=jnp.float32), GRID_W)
    col = jnp.tile(jnp.arange(GRID_W, dtype=jnp.float32), n_rows)
    inv = ROPE_THETA ** (-jnp.arange(0, AXIS_DIM, 2, dtype=jnp.float32) / AXIS_DIM)
    return row[:, None] * inv, col[:, None] * inv


def rope_axis(xa, ang):
    cos = jnp.cos(ang)[:, None, :].astype(xa.dtype)
    sin = jnp.sin(ang)[:, None, :].astype(xa.dtype)
    x1, x2 = jnp.split(xa, 2, axis=-1)
    return jnp.concatenate([x1 * cos - x2 * sin, x2 * cos + x1 * sin], axis=-1)


def axial_rope(x, ang_row, ang_col):
    return jnp.concatenate([rope_axis(x[..., :AXIS_DIM], ang_row),
                            rope_axis(x[..., AXIS_DIM:], ang_col)], axis=-1)


def qkv_project(h, w_qkv, q_gain, k_gain):
    B, S, _ = h.shape
    qkv = h @ w_qkv
    q, k, v = jnp.split(qkv, [N_HEADS * HEAD_DIM, (N_HEADS + N_KV_HEADS) * HEAD_DIM], axis=-1)
    q = rmsnorm(q.reshape(B, S, N_HEADS, HEAD_DIM), q_gain)
    k = rmsnorm(k.reshape(B, S, N_KV_HEADS, HEAD_DIM), k_gain)
    v = v.reshape(B, S, N_KV_HEADS, HEAD_DIM)
    return q, k, v


def blocked_attention(q, k, v):
    B, S, H, Dh = q.shape
    G = H // N_KV_HEADS
    qb = q.reshape(B, S // Q_BLOCK, Q_BLOCK, N_KV_HEADS, G, Dh).transpose(1, 0, 2, 3, 4, 5)
    scale = Dh ** -0.5

    def one_block(qblk):
        s = jnp.einsum('bqhgd,bkhd->bhgqk', qblk, k).astype(jnp.float32) * scale
        p = jax.nn.softmax(s, axis=-1).astype(v.dtype)
        return jnp.einsum('bhgqk,bkhd->bqhgd', p, v)

    out = lax.map(one_block, qb)
    return out.transpose(1, 0, 2, 3, 4, 5).reshape(B, S, H * Dh)


def short_conv_mixer(h, w_bcx, conv_w, conv_b, w_out):
    gb, gc, xin = jnp.split(h @ w_bcx, 3, axis=-1)
    u = gc * xin
    up = jnp.pad(u, ((0, 0), (1, 1), (0, 0)))
    S = u.shape[1]
    conv = (up[:, 0:S] * conv_w[0] + up[:, 1:S + 1] * conv_w[1]
            + up[:, 2:S + 2] * conv_w[2] + conv_b)
    return (gb * conv) @ w_out


def peer(h, w_pq, sub_keys, u_exp, v_exp):
    B, S, D = h.shape
    hb = h.reshape((B * S) // TOK_BLOCK, TOK_BLOCK, D)

    def one_block(xb):
        q = (xb @ w_pq).reshape(TOK_BLOCK, PEER_HEADS, 2, D_KEY_HALF)
        s = jnp.einsum('thcd,hcnd->thcn', q, sub_keys).astype(jnp.float32)
        s1, i1 = lax.top_k(s[:, :, 0], PEER_TOPK)
        s2, i2 = lax.top_k(s[:, :, 1], PEER_TOPK)
        cand_s = (s1[..., :, None] + s2[..., None, :]).reshape(TOK_BLOCK, PEER_HEADS, PEER_TOPK * PEER_TOPK)
        cand_i = (i1[..., :, None] * N_KEYS + i2[..., None, :]).reshape(TOK_BLOCK, PEER_HEADS, PEER_TOPK * PEER_TOPK)
        top_s, pos = lax.top_k(cand_s, PEER_TOPK)
        eidx = jnp.take_along_axis(cand_i, pos, axis=-1)
        g = jax.nn.softmax(top_s, axis=-1)
        u = jnp.take(u_exp, eidx, axis=0)
        a = jax.nn.gelu(jnp.einsum('thkd,td->thk', u, xb).astype(jnp.float32))
        w = (g * a).astype(xb.dtype)
        return jnp.einsum('thk,thkd->td', w, jnp.take(v_exp, eidx, axis=0))

    return lax.map(one_block, hb).reshape(B, S, D)


def setup_inputs(seed: int = 0) -> dict:
    key = jax.random.key(seed)
    ks = jax.random.split(key, 24)
    f32 = jnp.float32
    D = D_MODEL
    nrm = lambda k, shape, s: jax.random.normal(k, shape, f32) * s
    qkv_cols = (N_HEADS + 2 * N_KV_HEADS) * HEAD_DIM
    return {
        "x_prompt": nrm(ks[0], (BATCH, SEQ, D), 1.0),
        "x_sample": nrm(ks[1], (DEC_BATCH, DEC_SEQ, D), 1.0),
        "cache_k": nrm(ks[2], (DEC_BATCH, N_ATTN_LAYERS, PAST_LEN, N_KV_HEADS, HEAD_DIM), 1.0),
        "cache_v": nrm(ks[3], (DEC_BATCH, N_ATTN_LAYERS, PAST_LEN, N_KV_HEADS, HEAD_DIM), 1.0),
        "c": nrm(ks[4], (DEC_BATCH, D), 1.0),
        "c_ctx": nrm(ks[5], (D,), 1.0),
        "w_ada": nrm(ks[6], (DEPTH, D, N_MOD * D), 0.5 * D ** -0.5),
        "b_ada": nrm(ks[7], (DEPTH, N_MOD * D), 0.02),
        "norm_mix": 1.0 + nrm(ks[8], (DEPTH, D), 0.02),
        "norm_ffn": 1.0 + nrm(ks[9], (DEPTH, D), 0.02),
        "norm_final": 1.0 + nrm(ks[10], (D,), 0.02),
        "w_qkv": nrm(ks[11], (N_ATTN_LAYERS, D, qkv_cols), D ** -0.5),
        "q_gain": 1.0 + nrm(ks[12], (N_ATTN_LAYERS, HEAD_DIM), 0.02),
        "k_gain": 1.0 + nrm(ks[13], (N_ATTN_LAYERS, HEAD_DIM), 0.02),
        "w_o": nrm(ks[14], (N_ATTN_LAYERS, N_HEADS * HEAD_DIM, D), (N_HEADS * HEAD_DIM) ** -0.5),
        "w_bcx": nrm(ks[15], (N_CONV_LAYERS, D, 3 * D), D ** -0.5),
        "conv_w": nrm(ks[16], (N_CONV_LAYERS, CONV_W, D), CONV_W ** -0.5),
        "conv_b": nrm(ks[17], (N_CONV_LAYERS, D), 0.02),
        "w_conv_out": nrm(ks[18], (N_CONV_LAYERS, D, D), D ** -0.5),
        "w_pq": nrm(ks[19], (DEPTH, D, PEER_HEADS * D_KEY), D ** -0.5),
        "sub_keys": nrm(ks[20], (DEPTH, PEER_HEADS, 2, N_KEYS, D_KEY_HALF), D_KEY_HALF ** -0.5),
        "u_exp": nrm(ks[21], (DEPTH, N_EXPERTS, D), D ** -0.5),
        "v_exp": nrm(ks[22], (DEPTH, N_EXPERTS, D), PEER_HEADS ** -0.5),
    }


def reference(x_prompt, x_sample, cache_k, cache_v, c, c_ctx, w_ada, b_ada,
              norm_mix, norm_ffn, norm_final, w_qkv, q_gain, k_gain, w_o,
              w_bcx, conv_w, conv_b, w_conv_out, w_pq, sub_keys, u_exp, v_exp):
    xp = x_prompt
    xs = x_sample
    ang_row, ang_col = axial_angles(xs.shape[1])
    new_k, new_v = [], []
    for i in range(DEPTH):
        j = i // 2
        sh_m_c, sc_m_c, g_m_c, sh_f_c, sc_f_c, g_f_c = modulation(c_ctx[None, :], w_ada[i], b_ada[i])
        sh_m_s, sc_m_s, g_m_s, sh_f_s, sc_f_s, g_f_s = modulation(c, w_ada[i], b_ada[i])
        hp = modulate(rmsnorm(xp, norm_mix[i]), sh_m_c, sc_m_c)
        hs = modulate(rmsnorm(xs, norm_mix[i]), sh_m_s, sc_m_s)
        if i % 2 == 0:
            qc, kc, vc = qkv_project(hp, w_qkv[j], q_gain[j], k_gain[j])
            new_k.append(kc)
            new_v.append(vc)
            mp = blocked_attention(qc, kc, vc) @ w_o[j]
            qs, ks_, vs = qkv_project(hs, w_qkv[j], q_gain[j], k_gain[j])
            qs = axial_rope(qs, ang_row, ang_col)
            ks_ = axial_rope(ks_, ang_row, ang_col)
            k_all = jnp.concatenate([ks_, cache_k[:, j]], axis=1)
            v_all = jnp.concatenate([vs, cache_v[:, j]], axis=1)
            ms = blocked_attention(qs, k_all, v_all) @ w_o[j]
        else:
            mp = short_conv_mixer(hp, w_bcx[j], conv_w[j], conv_b[j], w_conv_out[j])
            ms = short_conv_mixer(hs, w_bcx[j], conv_w[j], conv_b[j], w_conv_out[j])
        xp = xp + g_m_c * mp
        xs = xs + g_m_s * ms
        hp = modulate(rmsnorm(xp, norm_ffn[i]), sh_f_c, sc_f_c)
        hs = modulate(rmsnorm(xs, norm_ffn[i]), sh_f_s, sc_f_s)
        xp = xp + g_f_c * peer(hp, w_pq[i], sub_keys[i], u_exp[i], v_exp[i])
        xs = xs + g_f_s * peer(hs, w_pq[i], sub_keys[i], u_exp[i], v_exp[i])
    y_prompt = rmsnorm(xp, norm_final)
    y_sample = rmsnorm(xs, norm_final)
    new_cache_k = jnp.stack(new_k, axis=1)
    new_cache_v = jnp.stack(new_v, axis=1)
    return (y_prompt, y_sample, new_cache_k, new_cache_v)
```

```python
import functools
import math

import jax
import jax.numpy as jnp
from jax import lax
from jax.experimental import pallas as pl
from jax.experimental.pallas import tpu as pltpu

F32 = jnp.float32
BF16 = jnp.bfloat16

GRID_W = 64
N_HEADS = 16
N_KV_HEADS = 4
HEAD_DIM = 64
ROT_HALF = 16
ROPE_THETA = 10000.0
PEER_HEADS = 8
N_KEYS = 128
PEER_TOPK = 16
N_MOD = 6
EPS = 1e-6
N_COND_ROWS = 8

TOKEN_BLOCK = 256
DENSE_TOKENS = 512
DENSE_EXPERTS = 1024
LANE_GROUP = 128
VMEM_LIMIT = 48 * 1024 * 1024


def _cparams(*sem):
    return pltpu.CompilerParams(dimension_semantics=sem, vmem_limit_bytes=VMEM_LIMIT)


def _split_bf16(a):
    hi = a.astype(BF16)
    lo = (a - hi.astype(F32)).astype(BF16)
    return hi, lo


def _norm_mod(x, nw, shift, scale):
    ms = jnp.mean(x * x, axis=-1, keepdims=True)
    return (x * lax.rsqrt(ms + EPS)) * nw * (1.0 + scale) + shift


def _dot_nt(a, b):
    return lax.dot_general(a, b, (((1,), (1,)), ((), ())), preferred_element_type=F32)


def _dot_tn(a, b):
    return lax.dot_general(a, b, (((0,), (0,)), ((), ())), preferred_element_type=F32)


def _mod_kernel(c_ref, w_ref, b_ref, o_ref):
    c = c_ref[...]
    s_hi, s_lo = _split_bf16(jax.nn.silu(c))
    w_hi, w_lo = _split_bf16(w_ref[...])
    acc = jnp.dot(s_hi, w_hi, preferred_element_type=F32)
    acc += jnp.dot(s_hi, w_lo, preferred_element_type=F32)
    acc += jnp.dot(s_lo, w_hi, preferred_element_type=F32)
    o_ref[...] = acc + b_ref[...]


def _modulation(cond, w_ada, b_ada):
    depth, d, nd = w_ada.shape
    nb = 1536
    return pl.pallas_call(
        _mod_kernel,
        grid=(depth, nd // nb),
        in_specs=[
            pl.BlockSpec((N_COND_ROWS, d), lambda l, n: (0, 0)),
            pl.BlockSpec((None, d, nb), lambda l, n: (l, 0, n)),
            pl.BlockSpec((None, 1, nb), lambda l, n: (l, 0, n)),
        ],
        out_specs=pl.BlockSpec((None, N_COND_ROWS, nb), lambda l, n: (l, 0, n)),
        out_shape=jax.ShapeDtypeStruct((depth, N_COND_ROWS, nd), F32),
        compiler_params=_cparams("parallel", "parallel"),
        name="adaln_modulation",
    )(cond, w_ada, b_ada.reshape(depth, 1, nd))


def _fold_kernel(sk_ref, w_ref, o_ref):
    k_hi, k_lo = _split_bf16(sk_ref[...])
    w_hi, w_lo = _split_bf16(w_ref[...])
    acc = _dot_nt(k_hi, w_hi) + _dot_nt(k_hi, w_lo) + _dot_nt(k_lo, w_hi)
    o_ref[...] = acc.astype(BF16)


def _fold_keys(w_pq, sub_keys):
    depth, d, _ = w_pq.shape
    dkh = sub_keys.shape[-1]
    n_parts = PEER_HEADS * 2
    return pl.pallas_call(
        _fold_kernel,
        grid=(depth, n_parts),
        in_specs=[
            pl.BlockSpec((None, None, None, N_KEYS, dkh), lambda l, p: (l, p // 2, p % 2, 0, 0)),
            pl.BlockSpec((None, d, dkh), lambda l, p: (l, 0, p)),
        ],
        out_specs=pl.BlockSpec((None, N_KEYS, d), lambda l, p: (l, p, 0)),
        out_shape=jax.ShapeDtypeStruct((depth, n_parts * N_KEYS, d), BF16),
        compiler_params=_cparams("parallel", "parallel"),
        name="peer_fold_keys",
    )(sub_keys, w_pq)


def _head_rms_scale(q, sel, sel_t):
    hi, lo = _split_bf16(q * q)
    ss = jnp.dot(hi, sel, preferred_element_type=F32) + jnp.dot(lo, sel, preferred_element_type=F32)
    r = lax.rsqrt(ss * (1.0 / HEAD_DIM) + EPS)
    r_hi, r_lo = _split_bf16(r)
    return jnp.dot(r_hi, sel_t, preferred_element_type=F32) + jnp.dot(r_lo, sel_t, preferred_element_type=F32)


def _rope(x, cos, sin_signed):
    n = x.shape[1]
    lane = lax.broadcasted_iota(jnp.int32, x.shape, 1)
    upper = (lane & ROT_HALF) != 0
    partner = jnp.where(upper, pltpu.roll(x, ROT_HALF, axis=1), pltpu.roll(x, n - ROT_HALF, axis=1))
    return x * cos + partner * sin_signed


def _qkv_kernel(x_ref, mod_ref, nw_ref, w_ref, qg_ref, kg_ref, selq_ref, selqt_ref, selk_ref, selkt_ref,
                cos_ref, sin_ref, q_ref, k_ref, v_ref, kf_ref, vf_ref):
    dq = N_HEADS * HEAD_DIM
    dk = N_KV_HEADS * HEAD_DIM
    h = _norm_mod(x_ref[...], nw_ref[...], mod_ref[0:1, :], mod_ref[1:2, :]).astype(BF16)
    qkv = jnp.dot(h, w_ref[...], preferred_element_type=F32)
    q = qkv[:, :dq]
    k = qkv[:, dq:dq + dk]
    v = qkv[:, dq + dk:]
    cos = cos_ref[...]
    sin = sin_ref[...]
    qn = q * _head_rms_scale(q, selq_ref[...], selqt_ref[...]) * qg_ref[...]
    q_ref[...] = _rope(qn, cos, sin).astype(BF16)
    kn = k * _head_rms_scale(k, selk_ref[...], selkt_ref[...]) * kg_ref[...]
    kf_ref[...] = kn
    vf_ref[...] = v
    kr = _rope(kn, cos[:, :dk], sin[:, :dk]).astype(BF16)
    vb = v.astype(BF16)
    for g in range(N_KV_HEADS):
        k_ref[g] = kr[:, g * HEAD_DIM:(g + 1) * HEAD_DIM]
        v_ref[g] = vb[:, g * HEAD_DIM:(g + 1) * HEAD_DIM]


def _softmax_pv(qh, parts):
    scores = [_dot_nt(qh, k) for k, _ in parts]
    m = scores[0].max(axis=-1, keepdims=True)
    for s in scores[1:]:
        m = jnp.maximum(m, s.max(axis=-1, keepdims=True))
    num = None
    den = None
    for s, (_, v) in zip(scores, parts):
        p = jnp.exp(s - m)
        d = p.sum(axis=-1, keepdims=True)
        o = jnp.dot(p.astype(BF16), v, preferred_element_type=F32)
        num = o if num is None else num + o
        den = d if den is None else den + d
    return num / den


def _attn_kernel(q_ref, ka_ref, va_ref, kb_ref, vb_ref, kc_ref, vc_ref, o_ref, *, n_prompt_blocks):
    u = pl.program_id(0)
    group = N_HEADS // N_KV_HEADS

    @pl.when(u < n_prompt_blocks)
    def _():
        parts = [(ka_ref[...], va_ref[...])]
        for hh in range(group):
            sl = slice(hh * HEAD_DIM, (hh + 1) * HEAD_DIM)
            o_ref[:, sl] = _softmax_pv(q_ref[:, sl], parts).astype(BF16)

    @pl.when(u >= n_prompt_blocks)
    def _():
        parts = [(kb_ref[...], vb_ref[...]), (kc_ref[...], vc_ref[...])]
        for hh in range(group):
            sl = slice(hh * HEAD_DIM, (hh + 1) * HEAD_DIM)
            o_ref[:, sl] = _softmax_pv(q_ref[:, sl], parts).astype(BF16)


def _proj_res_kernel(a_ref, w_ref, x_ref, mod_ref, o_ref):
    o_ref[...] = x_ref[...] + mod_ref[2:3, :] * jnp.dot(a_ref[...], w_ref[...], preferred_element_type=F32)


def _conv_in_kernel(x_ref, mod_ref, nw_ref, w_ref, u_ref, gb_ref):
    d = x_ref.shape[1]
    h = _norm_mod(x_ref[...], nw_ref[...], mod_ref[0:1, :], mod_ref[1:2, :]).astype(BF16)
    bcx = jnp.dot(h, w_ref[...], preferred_element_type=F32)
    gb_ref[...] = bcx[:, :d].astype(BF16)
    u_ref[...] = bcx[:, d:2 * d] * bcx[:, 2 * d:]


def _conv_out_kernel(u_ref, up_ref, un_ref, gb_ref, cw_ref, cb_ref, w_ref, x_ref, mod_ref, o_ref, *,
                     n_prompt_blocks, blocks_per_prompt_seq, blocks_per_sample_seq):
    tb = u_ref.shape[0]
    b = pl.program_id(0)
    bs = b - n_prompt_blocks
    pos = jnp.where(b < n_prompt_blocks, b % blocks_per_prompt_seq, bs % blocks_per_sample_seq)
    per_seq = jnp.where(b < n_prompt_blocks, blocks_per_prompt_seq, blocks_per_sample_seq)
    u = u_ref[...]
    prev_row = jnp.where(pos == 0, 0.0, up_ref[7:8, :])
    next_row = jnp.where(pos == per_seq - 1, 0.0, un_ref[0:1, :])
    row = lax.broadcasted_iota(jnp.int32, u.shape, 0)
    u_m1 = jnp.where(row == 0, prev_row, pltpu.roll(u, 1, axis=0))
    u_p1 = jnp.where(row == tb - 1, next_row, pltpu.roll(u, tb - 1, axis=0))
    conv = u_m1 * cw_ref[0:1, :] + u * cw_ref[1:2, :] + u_p1 * cw_ref[2:3, :] + cb_ref[...]
    y = (gb_ref[...].astype(F32) * conv).astype(BF16)
    o_ref[...] = x_ref[...] + mod_ref[2:3, :] * jnp.dot(y, w_ref[...], preferred_element_type=F32)


def _extract_top(x, count):
    nrows = float(x.shape[0])
    row = lax.broadcasted_iota(jnp.int32, x.shape, 0).astype(F32)
    out = []
    for _ in range(count):
        m = jnp.max(x, axis=0, keepdims=True)
        first = jnp.min(jnp.where(x == m, row, nrows), axis=0, keepdims=True)
        out.append(m)
        x = jnp.where(row == first, -jnp.inf, x)
    return out


def _route_kernel(x_ref, mod_ref, nw_ref, wf_ref, h_ref, n_ref, e1_ref, r2_ref, e2_ref):
    h = _norm_mod(x_ref[...], nw_ref[...], mod_ref[3:4, :], mod_ref[4:5, :]).astype(BF16)
    h_ref[...] = h
    s_t = _dot_nt(wf_ref[...], h)
    t = s_t.shape[1]
    for hd in range(PEER_HEADS):
        s1 = s_t[(2 * hd) * N_KEYS:(2 * hd + 1) * N_KEYS]
        s2 = s_t[(2 * hd + 1) * N_KEYS:(2 * hd + 2) * N_KEYS]
        a = _extract_top(s1, PEER_TOPK)
        b = _extract_top(s2, PEER_TOPK)
        cand = []
        for p in range(1, PEER_TOPK + 1):
            for q in range(1, PEER_TOPK // p + 1):
                cand.append(a[p - 1] + b[q - 1])
        pad = (-len(cand)) % 8
        cand += [jnp.full((1, t), -jnp.inf, F32)] * pad
        top = _extract_top(jnp.concatenate(cand, axis=0), PEER_TOPK)
        tau = top[-1]
        z = jnp.zeros((1, t), F32)
        for v in top:
            z = z + jnp.exp(v - top[0])
        n = jnp.zeros(s1.shape, F32)
        r2 = jnp.zeros(s2.shape, F32)
        for q in range(PEER_TOPK):
            n = n + jnp.where(s1 + b[q] >= tau, 1.0, 0.0)
            r2 = r2 + jnp.where(b[q] > s2, 1.0, 0.0)
        n_ref[hd] = n
        e1_ref[hd] = jnp.exp(s1 - a[0]) / z
        r2_ref[hd] = r2.astype(BF16)
        e2_ref[hd] = jnp.exp(s2 - b[0]).astype(BF16)


def _dense_kernel(h_ref, u_ref, v_ref, n_ref, e1_ref, r2_ref, e2_ref, x_ref, mod_ref, o_ref,
                  acc_ref, a_ref, p_ref):
    c = pl.program_id(1)
    n_rows = u_ref.shape[0] // N_KEYS
    tb = h_ref.shape[0]
    sub = 16

    @pl.when(c == 0)
    def _():
        acc_ref[...] = jnp.zeros_like(acc_ref)

    a_ref[...] = _dot_nt(u_ref[...], h_ref[...])

    def body(l, carry):
        ls = pl.ds(pl.multiple_of(l * LANE_GROUP, LANE_GROUP), LANE_GROUP)
        for ii in range(n_rows):
            rs = slice(ii * N_KEYS, (ii + 1) * N_KEYS)
            w = jnp.zeros((N_KEYS // sub, sub, LANE_GROUP), BF16)
            for hd in range(PEER_HEADS):
                nb = jnp.broadcast_to(n_ref[hd, ii:ii + 1, ls], (sub, LANE_GROUP)).astype(BF16)
                eb = jnp.broadcast_to(e1_ref[hd, ii:ii + 1, ls], (sub, LANE_GROUP)).astype(BF16)
                r2 = r2_ref[hd, :, ls].reshape(N_KEYS // sub, sub, LANE_GROUP)
                e2 = e2_ref[hd, :, ls].reshape(N_KEYS // sub, sub, LANE_GROUP)
                w = w + jnp.where(r2 < nb[None], e2 * eb[None], jnp.zeros_like(e2))
            act = jax.nn.gelu(a_ref[rs, ls]).astype(BF16)
            p_ref[rs, ls] = act * w.reshape(N_KEYS, LANE_GROUP)
        return carry

    lax.fori_loop(0, tb // LANE_GROUP, body, 0)
    acc_ref[...] += _dot_tn(p_ref[...], v_ref[...])

    @pl.when(c == pl.num_programs(1) - 1)
    def _():
        o_ref[...] = x_ref[...] + mod_ref[5:6, :] * acc_ref[...]


def _final_norm_kernel(x_ref, nw_ref, o_ref):
    x = x_ref[...]
    ms = jnp.mean(x * x, axis=-1, keepdims=True)
    o_ref[...] = x * lax.rsqrt(ms + EPS) * nw_ref[...]


def _rope_tables(seq_len, n_identity):
    axis_dim = HEAD_DIM // 2
    n_rows = seq_len // GRID_W
    row = jnp.repeat(jnp.arange(n_rows, dtype=F32), GRID_W)
    col = jnp.tile(jnp.arange(GRID_W, dtype=F32), n_rows)
    inv = ROPE_THETA ** (-jnp.arange(0, axis_dim, 2, dtype=F32) / axis_dim)
    ang_r = row[:, None] * inv
    ang_c = col[:, None] * inv
    cos = jnp.concatenate([jnp.cos(ang_r)] * 2 + [jnp.cos(ang_c)] * 2, axis=-1)
    sin = jnp.concatenate([-jnp.sin(ang_r), jnp.sin(ang_r), -jnp.sin(ang_c), jnp.sin(ang_c)], axis=-1)
    cos = jnp.concatenate([jnp.ones((n_identity, HEAD_DIM), F32), cos], axis=0)
    sin = jnp.concatenate([jnp.zeros((n_identity, HEAD_DIM), F32), sin], axis=0)
    return jnp.tile(cos, (1, N_HEADS)), jnp.tile(sin, (1, N_HEADS))


def _head_selectors(n_heads):
    c = jnp.arange(n_heads * HEAD_DIM)[:, None] // HEAD_DIM
    sel = (c == jnp.arange(128)[None, :]).astype(BF16)
    return sel, sel.T


def kernel(x_prompt, x_sample, cache_k, cache_v, c, c_ctx, w_ada, b_ada, norm_mix, norm_ffn, norm_final,
           w_qkv, q_gain, k_gain, w_o, w_bcx, conv_w, conv_b, w_conv_out, w_pq, sub_keys, u_exp, v_exp):
    batch, seq, d = x_prompt.shape
    dec_batch, dec_seq, _ = x_sample.shape
    depth = w_ada.shape[0]
    past = cache_k.shape[2]
    tp = batch * seq
    ts = dec_batch * dec_seq
    tt = tp + ts
    tb = TOKEN_BLOCK
    assert seq % tb == 0 and dec_seq % tb == 0 and dec_batch + 1 <= N_COND_ROWS
    assert tp % DENSE_TOKENS == 0 and dec_seq % DENSE_TOKENS == 0
    assert d == N_HEADS * HEAD_DIM and seq == tb and past == tb and tp % dec_seq == 0
    npb = tp // tb
    bpp = seq // tb
    bps = dec_seq // tb
    nblk = tt // tb
    dq = N_HEADS * HEAD_DIM
    dk = N_KV_HEADS * HEAD_DIM
    n_exp = u_exp.shape[1]

    def cond_row(b, rows=tb):
        t0 = b * rows
        return jnp.where(t0 < tp, 0, 1 + (t0 - tp) // dec_seq)

    def mod_spec(layer, rows=tb, grid_rank=1):
        if grid_rank == 1:
            return pl.BlockSpec((None, None, N_MOD, d), lambda b: (layer, cond_row(b, rows), 0, 0))
        return pl.BlockSpec((None, None, N_MOD, d), lambda b, e: (layer, cond_row(b, rows), 0, 0))

    def row_spec(width, dtype_rows=tb):
        return pl.BlockSpec((dtype_rows, width), lambda b: (b, 0))

    def layer_vec(layer, width):
        return pl.BlockSpec((None, 1, width), lambda b: (layer, 0, 0))

    def const2(shape):
        return pl.BlockSpec(shape, lambda b: (0, 0))

    x = jnp.concatenate([x_prompt.reshape(tp, d), x_sample.reshape(ts, d)], axis=0)
    cond = jnp.zeros((N_COND_ROWS, d), F32).at[0].set(c_ctx).at[1:1 + dec_batch].set(c)
    mods = _modulation(cond, w_ada, b_ada).reshape(depth, N_COND_ROWS, N_MOD, d)

    wf_all = _fold_keys(w_pq, sub_keys)
    u_bf = u_exp.astype(BF16)
    v_bf = v_exp.astype(BF16)
    w_qkv_bf = w_qkv.astype(BF16)
    w_o_bf = w_o.astype(BF16)
    w_bcx_bf = w_bcx.astype(BF16)
    w_co_bf = w_conv_out.astype(BF16)
    norm_mix3 = norm_mix.reshape(depth, 1, d)
    norm_ffn3 = norm_ffn.reshape(depth, 1, d)

    cos_t, sin_t = _rope_tables(dec_seq, tb)
    selq, selq_t = _head_selectors(N_HEADS)
    selk, selk_t = _head_selectors(N_KV_HEADS)
    scale = HEAD_DIM ** -0.5
    qg_all = jnp.tile(q_gain, (1, N_HEADS)).reshape(-1, 1, dq) * scale
    kg_all = jnp.tile(k_gain, (1, N_KV_HEADS)).reshape(-1, 1, dk)
    ck_all = jnp.transpose(cache_k, (1, 0, 3, 2, 4)).astype(BF16)
    cv_all = jnp.transpose(cache_v, (1, 0, 3, 2, 4)).astype(BF16)

    def rope_block(b):
        return jnp.where(b < npb, 0, 1 + (b - npb) % bps)

    new_k, new_v = [], []
    for i in range(depth):
        j = i // 2
        if i % 2 == 0:
            q, k_hm, v_hm, kf, vf = pl.pallas_call(
                _qkv_kernel,
                grid=(nblk,),
                in_specs=[
                    row_spec(d), mod_spec(i), layer_vec(i, d),
                    pl.BlockSpec((None, d, dq + 2 * dk), lambda b, j=j: (j, 0, 0)),
                    pl.BlockSpec((None, 1, dq), lambda b, j=j: (j, 0, 0)),
                    pl.BlockSpec((None, 1, dk), lambda b, j=j: (j, 0, 0)),
                    const2((dq, 128)), const2((128, dq)), const2((dk, 128)), const2((128, dk)),
                    pl.BlockSpec((tb, dq), lambda b: (rope_block(b), 0)),
                    pl.BlockSpec((tb, dq), lambda b: (rope_block(b), 0)),
                ],
                out_specs=[
                    row_spec(dq),
                    pl.BlockSpec((N_KV_HEADS, tb, HEAD_DIM), lambda b: (0, b, 0)),
                    pl.BlockSpec((N_KV_HEADS, tb, HEAD_DIM), lambda b: (0, b, 0)),
                    row_spec(dk), row_spec(dk),
                ],
                out_shape=[
                    jax.ShapeDtypeStruct((tt, dq), BF16),
                    jax.ShapeDtypeStruct((N_KV_HEADS, tt, HEAD_DIM), BF16),
                    jax.ShapeDtypeStruct((N_KV_HEADS, tt, HEAD_DIM), BF16),
                    jax.ShapeDtypeStruct((tt, dk), F32),
                    jax.ShapeDtypeStruct((tt, dk), F32),
                ],
                compiler_params=_cparams("parallel"),
                name="qkv_project",
            )(x, mods, norm_mix3, w_qkv_bf, qg_all, kg_all, selq, selq_t, selk, selk_t, cos_t, sin_t)
            new_k.append(kf[:tp].reshape(batch, seq, N_KV_HEADS, HEAD_DIM))
            new_v.append(vf[:tp].reshape(batch, seq, N_KV_HEADS, HEAD_DIM))

            def sample_seq(u):
                return jnp.maximum(u - npb, 0) // bps

            kv_a = pl.BlockSpec((None, tb, HEAD_DIM), lambda u, g: (g, jnp.minimum(u, npb - 1), 0))
            kv_b = pl.BlockSpec((None, dec_seq, HEAD_DIM), lambda u, g: (g, tp // dec_seq + sample_seq(u), 0))
            kv_c = pl.BlockSpec((None, None, None, past, HEAD_DIM), lambda u, g, j=j: (j, sample_seq(u), g, 0, 0))
            attn = pl.pallas_call(
                functools.partial(_attn_kernel, n_prompt_blocks=npb),
                grid=(nblk, N_KV_HEADS),
                in_specs=[pl.BlockSpec((tb, dk), lambda u, g: (u, g)), kv_a, kv_a, kv_b, kv_b, kv_c, kv_c],
                out_specs=pl.BlockSpec((tb, dk), lambda u, g: (u, g)),
                out_shape=jax.ShapeDtypeStruct((tt, dq), BF16),
                compiler_params=_cparams("parallel", "parallel"),
                name="attention",
            )(q, k_hm, v_hm, k_hm, v_hm, ck_all, cv_all)

            x = pl.pallas_call(
                _proj_res_kernel,
                grid=(nblk,),
                in_specs=[row_spec(dq), pl.BlockSpec((None, dq, d), lambda b, j=j: (j, 0, 0)), row_spec(d), mod_spec(i)],
                out_specs=row_spec(d),
                out_shape=jax.ShapeDtypeStruct((tt, d), F32),
                compiler_params=_cparams("parallel"),
                name="attn_out_project",
            )(attn, w_o_bf, x, mods)
        else:
            u, gb = pl.pallas_call(
                _conv_in_kernel,
                grid=(nblk,),
                in_specs=[row_spec(d), mod_spec(i), layer_vec(i, d),
                          pl.BlockSpec((None, d, 3 * d), lambda b, j=j: (j, 0, 0))],
                out_specs=[row_spec(d), row_spec(d)],
                out_shape=[jax.ShapeDtypeStruct((tt, d), F32), jax.ShapeDtypeStruct((tt, d), BF16)],
                compiler_params=_cparams("parallel"),
                name="conv_in_project",
            )(x, mods, norm_mix3, w_bcx_bf)
            halo = tb // 8
            x = pl.pallas_call(
                functools.partial(_conv_out_kernel, n_prompt_blocks=npb, blocks_per_prompt_seq=bpp,
                                  blocks_per_sample_seq=bps),
                grid=(nblk,),
                in_specs=[
                    row_spec(d),
                    pl.BlockSpec((8, d), lambda b: (jnp.maximum(b * halo - 1, 0), 0)),
                    pl.BlockSpec((8, d), lambda b: (jnp.minimum((b + 1) * halo, nblk * halo - 1), 0)),
                    row_spec(d),
                    pl.BlockSpec((None, 3, d), lambda b, j=j: (j, 0, 0)),
                    pl.BlockSpec((None, 1, d), lambda b, j=j: (j, 0, 0)),
                    pl.BlockSpec((None, d, d), lambda b, j=j: (j, 0, 0)),
                    row_spec(d), mod_spec(i),
                ],
                out_specs=row_spec(d),
                out_shape=jax.ShapeDtypeStruct((tt, d), F32),
                compiler_params=_cparams("parallel"),
                name="conv_out_project",
            )(u, u, u, gb, conv_w, conv_b.reshape(-1, 1, d), w_co_bf, x, mods)

        n_keys2 = PEER_HEADS * 2 * N_KEYS
        tab = lambda dt: jax.ShapeDtypeStruct((PEER_HEADS, N_KEYS, tt), dt)
        tab_spec = pl.BlockSpec((PEER_HEADS, N_KEYS, tb), lambda b: (0, 0, b))
        h, n_tab, e1_tab, r2_tab, e2_tab = pl.pallas_call(
            _route_kernel,
            grid=(nblk,),
            in_specs=[row_spec(d), mod_spec(i), layer_vec(i, d),
                      pl.BlockSpec((None, n_keys2, d), lambda b, i=i: (i, 0, 0))],
            out_specs=[row_spec(d), tab_spec, tab_spec, tab_spec, tab_spec],
            out_shape=[jax.ShapeDtypeStruct((tt, d), BF16), tab(F32), tab(F32), tab(BF16), tab(BF16)],
            compiler_params=_cparams("parallel"),
            name="peer_route",
        )(x, mods, norm_ffn3, wf_all)

        dt, de = DENSE_TOKENS, DENSE_EXPERTS
        dtab = pl.BlockSpec((PEER_HEADS, N_KEYS, dt), lambda b, e: (0, 0, b))
        rtab = pl.BlockSpec((PEER_HEADS, de // N_KEYS, dt), lambda b, e: (0, e, b))
        x = pl.pallas_call(
            _dense_kernel,
            grid=(tt // dt, n_exp // de),
            in_specs=[
                pl.BlockSpec((dt, d), lambda b, e: (b, 0)),
                pl.BlockSpec((None, de, d), lambda b, e, i=i: (i, e, 0)),
                pl.BlockSpec((None, de, d), lambda b, e, i=i: (i, e, 0)),
                rtab, rtab, dtab, dtab,
                pl.BlockSpec((dt, d), lambda b, e: (b, 0)),
                mod_spec(i, rows=dt, grid_rank=2),
            ],
            out_specs=pl.BlockSpec((dt, d), lambda b, e: (b, 0)),
            out_shape=jax.ShapeDtypeStruct((tt, d), F32),
            scratch_shapes=[pltpu.VMEM((dt, d), F32), pltpu.VMEM((de, dt), F32), pltpu.VMEM((de, dt), BF16)],
            compiler_params=_cparams("parallel", "arbitrary"),
            name="peer_dense",
        )(h, u_bf, v_bf, n_tab, e1_tab, r2_tab, e2_tab, x, mods)

    y = pl.pallas_call(
        _final_norm_kernel,
        grid=(nblk,),
        in_specs=[row_spec(d), const2((1, d))],
        out_specs=row_spec(d),
        out_shape=jax.ShapeDtypeStruct((tt, d), F32),
        compiler_params=_cparams("parallel"),
        name="final_norm",
    )(x, norm_final.reshape(1, d))

    y_prompt = y[:tp].reshape(batch, seq, d)
    y_sample = y[tp:].reshape(dec_batch, dec_seq, d)
    return (y_prompt, y_sample, jnp.stack(new_k, axis=1), jnp.stack(new_v, axis=1))
```

```python
import functools

import jax
import jax.numpy as jnp
from jax import lax
from jax.experimental import pallas as pl
from jax.experimental.pallas import tpu as pltpu

F32 = jnp.float32
BF16 = jnp.bfloat16

GRID_W = 64
N_HEADS = 16
N_KV_HEADS = 4
HEAD_DIM = 64
ROT_HALF = 16
ROPE_THETA = 10000.0
PEER_HEADS = 8
N_KEYS = 128
PEER_TOPK = 16
N_MOD = 6
EPS = 1e-6
N_COND_ROWS = 8
SUBLANES = 8

TOKEN_BLOCK = 256
DENSE_TOKENS = 512
DENSE_CHUNK = 1024
MASK_LANES = 128
ROUTE_LANES = 128
VMEM_LIMIT = 48 * 1024 * 1024


def _cparams(*sem):
    return pltpu.CompilerParams(dimension_semantics=sem, vmem_limit_bytes=VMEM_LIMIT)


def _split_bf16(a):
    hi = a.astype(BF16)
    lo = (a - hi.astype(F32)).astype(BF16)
    return hi, lo


def _norm_mod(x, nw, shift, scale):
    ms = jnp.mean(x * x, axis=-1, keepdims=True)
    return (x * lax.rsqrt(ms + EPS)) * nw * (1.0 + scale) + shift


def _dot_nt(a, b):
    return lax.dot_general(a, b, (((1,), (1,)), ((), ())), preferred_element_type=F32)


def _dot_tn(a, b):
    return lax.dot_general(a, b, (((0,), (0,)), ((), ())), preferred_element_type=F32)


def _mod_kernel(c_ref, w_ref, b_ref, o_ref):
    c = c_ref[...]
    s_hi, s_lo = _split_bf16(jax.nn.silu(c))
    w_hi, w_lo = _split_bf16(w_ref[...])
    acc = jnp.dot(s_hi, w_hi, preferred_element_type=F32)
    acc += jnp.dot(s_hi, w_lo, preferred_element_type=F32)
    acc += jnp.dot(s_lo, w_hi, preferred_element_type=F32)
    o_ref[...] = acc + b_ref[...]


def _modulation(cond, w_ada, b_ada):
    depth, d, nd = w_ada.shape
    nb = 1536
    return pl.pallas_call(
        _mod_kernel,
        grid=(depth, nd // nb),
        in_specs=[
            pl.BlockSpec((N_COND_ROWS, d), lambda l, n: (0, 0)),
            pl.BlockSpec((None, d, nb), lambda l, n: (l, 0, n)),
            pl.BlockSpec((None, 1, nb), lambda l, n: (l, 0, n)),
        ],
        out_specs=pl.BlockSpec((None, N_COND_ROWS, nb), lambda l, n: (l, 0, n)),
        out_shape=jax.ShapeDtypeStruct((depth, N_COND_ROWS, nd), F32),
        compiler_params=_cparams("parallel", "parallel"),
        name="adaln_modulation",
    )(cond, w_ada, b_ada.reshape(depth, 1, nd))


def _fold_kernel(sk_ref, w_ref, o_ref):
    k_hi, k_lo = _split_bf16(sk_ref[...])
    w_hi, w_lo = _split_bf16(w_ref[...])
    acc = _dot_nt(k_hi, w_hi) + _dot_nt(k_hi, w_lo) + _dot_nt(k_lo, w_hi)
    o_ref[...] = acc.astype(BF16)


def _fold_keys(w_pq, sub_keys):
    depth, d, _ = w_pq.shape
    dkh = sub_keys.shape[-1]
    n_parts = PEER_HEADS * 2
    return pl.pallas_call(
        _fold_kernel,
        grid=(depth, n_parts),
        in_specs=[
            pl.BlockSpec((None, None, None, N_KEYS, dkh), lambda l, p: (l, p // 2, p % 2, 0, 0)),
            pl.BlockSpec((None, d, dkh), lambda l, p: (l, 0, p)),
        ],
        out_specs=pl.BlockSpec((None, N_KEYS, d), lambda l, p: (l, p, 0)),
        out_shape=jax.ShapeDtypeStruct((depth, n_parts * N_KEYS, d), BF16),
        compiler_params=_cparams("parallel", "parallel"),
        name="peer_fold_keys",
    )(sub_keys, w_pq)


def _head_rms_scale(q, sel, sel_t):
    hi, lo = _split_bf16(q * q)
    ss = jnp.dot(hi, sel, preferred_element_type=F32) + jnp.dot(lo, sel, preferred_element_type=F32)
    r = lax.rsqrt(ss * (1.0 / HEAD_DIM) + EPS)
    r_hi, r_lo = _split_bf16(r)
    return jnp.dot(r_hi, sel_t, preferred_element_type=F32) + jnp.dot(r_lo, sel_t, preferred_element_type=F32)


def _rope(x, cos, sin_signed):
    n = x.shape[1]
    lane = lax.broadcasted_iota(jnp.int32, x.shape, 1)
    upper = (lane & ROT_HALF) != 0
    partner = jnp.where(upper, pltpu.roll(x, ROT_HALF, axis=1), pltpu.roll(x, n - ROT_HALF, axis=1))
    return x * cos + partner * sin_signed


def _qkv_kernel(x_ref, mod_ref, nw_ref, w_ref, qg_ref, kg_ref, selq_ref, selqt_ref, selk_ref, selkt_ref,
                cos_ref, sin_ref, q_ref, k_ref, v_ref, kf_ref, vf_ref):
    dq = N_HEADS * HEAD_DIM
    dk = N_KV_HEADS * HEAD_DIM
    h = _norm_mod(x_ref[...], nw_ref[...], mod_ref[0:1, :], mod_ref[1:2, :]).astype(BF16)
    qkv = jnp.dot(h, w_ref[...], preferred_element_type=F32)
    q = qkv[:, :dq]
    k = qkv[:, dq:dq + dk]
    v = qkv[:, dq + dk:]
    cos = cos_ref[...]
    sin = sin_ref[...]
    qn = q * _head_rms_scale(q, selq_ref[...], selqt_ref[...]) * qg_ref[...]
    q_ref[...] = _rope(qn, cos, sin).astype(BF16)
    kn = k * _head_rms_scale(k, selk_ref[...], selkt_ref[...]) * kg_ref[...]
    kf_ref[...] = kn
    vf_ref[...] = v
    kr = _rope(kn, cos[:, :dk], sin[:, :dk]).astype(BF16)
    vb = v.astype(BF16)
    for g in range(N_KV_HEADS):
        k_ref[g] = kr[:, g * HEAD_DIM:(g + 1) * HEAD_DIM]
        v_ref[g] = vb[:, g * HEAD_DIM:(g + 1) * HEAD_DIM]


def _softmax_pv(qh, parts):
    scores = [_dot_nt(qh, k) for k, _ in parts]
    m = scores[0].max(axis=-1, keepdims=True)
    for s in scores[1:]:
        m = jnp.maximum(m, s.max(axis=-1, keepdims=True))
    num = None
    den = None
    for s, (_, v) in zip(scores, parts):
        p = jnp.exp(s - m)
        d = p.sum(axis=-1, keepdims=True)
        o = jnp.dot(p.astype(BF16), v, preferred_element_type=F32)
        num = o if num is None else num + o
        den = d if den is None else den + d
    return num / den


def _attn_kernel(q_ref, ka_ref, va_ref, kb_ref, vb_ref, kc_ref, vc_ref, o_ref, *, n_prompt_blocks):
    u = pl.program_id(0)
    group = N_HEADS // N_KV_HEADS

    @pl.when(u < n_prompt_blocks)
    def _():
        parts = [(ka_ref[...], va_ref[...])]
        for hh in range(group):
            sl = slice(hh * HEAD_DIM, (hh + 1) * HEAD_DIM)
            o_ref[:, sl] = _softmax_pv(q_ref[:, sl], parts).astype(BF16)

    @pl.when(u >= n_prompt_blocks)
    def _():
        parts = [(kb_ref[...], vb_ref[...]), (kc_ref[...], vc_ref[...])]
        for hh in range(group):
            sl = slice(hh * HEAD_DIM, (hh + 1) * HEAD_DIM)
            o_ref[:, sl] = _softmax_pv(q_ref[:, sl], parts).astype(BF16)


def _proj_res_kernel(a_ref, w_ref, x_ref, mod_ref, o_ref):
    o_ref[...] = x_ref[...] + mod_ref[2:3, :] * jnp.dot(a_ref[...], w_ref[...], preferred_element_type=F32)


def _conv_in_kernel(x_ref, mod_ref, nw_ref, w_ref, u_ref, gb_ref):
    d = x_ref.shape[1]
    h = _norm_mod(x_ref[...], nw_ref[...], mod_ref[0:1, :], mod_ref[1:2, :]).astype(BF16)
    bcx = jnp.dot(h, w_ref[...], preferred_element_type=F32)
    gb_ref[...] = bcx[:, :d].astype(BF16)
    u_ref[...] = bcx[:, d:2 * d] * bcx[:, 2 * d:]


def _conv_out_kernel(u_ref, up_ref, un_ref, gb_ref, cw_ref, cb_ref, w_ref, x_ref, mod_ref, o_ref, *,
                     n_prompt_blocks, blocks_per_prompt_seq, blocks_per_sample_seq):
    tb = u_ref.shape[0]
    b = pl.program_id(0)
    bs = b - n_prompt_blocks
    pos = jnp.where(b < n_prompt_blocks, b % blocks_per_prompt_seq, bs % blocks_per_sample_seq)
    per_seq = jnp.where(b < n_prompt_blocks, blocks_per_prompt_seq, blocks_per_sample_seq)
    u = u_ref[...]
    prev_row = jnp.where(pos == 0, 0.0, up_ref[7:8, :])
    next_row = jnp.where(pos == per_seq - 1, 0.0, un_ref[0:1, :])
    row = lax.broadcasted_iota(jnp.int32, u.shape, 0)
    u_m1 = jnp.where(row == 0, prev_row, pltpu.roll(u, 1, axis=0))
    u_p1 = jnp.where(row == tb - 1, next_row, pltpu.roll(u, tb - 1, axis=0))
    conv = u_m1 * cw_ref[0:1, :] + u * cw_ref[1:2, :] + u_p1 * cw_ref[2:3, :] + cb_ref[...]
    y = (gb_ref[...].astype(F32) * conv).astype(BF16)
    o_ref[...] = x_ref[...] + mod_ref[2:3, :] * jnp.dot(y, w_ref[...], preferred_element_type=F32)


def _oddeven_merge(lo, hi, r):
    step = r * 2
    if step < hi - lo:
        yield from _oddeven_merge(lo, hi, step)
        yield from _oddeven_merge(lo + r, hi, step)
        yield from [(i, i + r) for i in range(lo + r, hi - r, step)]
    else:
        yield (lo, lo + r)


def _oddeven_sort(lo, hi):
    if hi - lo >= 1:
        mid = lo + (hi - lo) // 2
        yield from _oddeven_sort(lo, mid)
        yield from _oddeven_sort(mid + 1, hi)
        yield from _oddeven_merge(lo, hi, 1)


def _bitonic_merge_pairs(n):
    out, s = [], n // 2
    while s >= 1:
        out += [(i, i + s) for i in range(n) if (i // s) % 2 == 0]
        s //= 2
    return out


SORT16 = tuple(_oddeven_sort(0, 15))
SORT8 = tuple(_oddeven_sort(0, 7))
BITONIC16 = tuple(_bitonic_merge_pairs(16))


def _apply_net(net, v):
    v = list(v)
    for i, j in net:
        hi, lo = jnp.maximum(v[i], v[j]), jnp.minimum(v[i], v[j])
        v[i], v[j] = hi, lo
    return v


def _merge_top16(a, b):
    return _apply_net(BITONIC16, [jnp.maximum(a[i], b[PEER_TOPK - 1 - i]) for i in range(PEER_TOPK)])


def _sorted_top16(x):
    v = _apply_net(SORT16, [x[SUBLANES * r:SUBLANES * (r + 1)] for r in range(N_KEYS // SUBLANES)])
    for shift in (4, 2, 1):
        v = _merge_top16(v, [pltpu.roll(u, shift, axis=0) for u in v])
    return v


def _route_tables(s1, s2):
    t = s1.shape[1]
    a = _sorted_top16(s1)
    b = _sorted_top16(s2)
    neg = jnp.full((SUBLANES, t), -jnp.inf, F32)
    c = {(p, q): a[p - 1] + b[q - 1] for p in range(1, PEER_TOPK + 1) for q in range(1, PEER_TOPK // p + 1)}
    row1 = [c[(1, q)] for q in range(1, 17)]
    col1 = [c[(p, 1)] for p in range(2, 17)] + [neg]
    row2 = [c[(2, q)] for q in range(2, 9)] + [neg]
    col2 = [c[(p, 2)] for p in range(3, 9)] + [neg, neg]
    rest = [c[(3, 3)], c[(3, 4)], c[(3, 5)], c[(4, 3)], c[(5, 3)], c[(4, 4)], neg, neg]
    m2 = _apply_net(BITONIC16, row2 + col2[::-1])
    m3 = _apply_net(SORT8, rest) + [neg] * 8
    top = _merge_top16(_merge_top16(row1, col1), _merge_top16(m2, m3))
    tau = top[PEER_TOPK - 1]
    z = jnp.zeros((SUBLANES, t), F32)
    for v in top:
        z = z + jnp.exp(v - top[0])
    inv_z = 1.0 / z
    n_of_rank = []
    for p in range(1, PEER_TOPK + 1):
        cnt = jnp.zeros((SUBLANES, t), F32)
        for q in range(1, PEER_TOPK // p + 1):
            cnt = cnt + jnp.where(c[(p, q)] >= tau, 1.0, 0.0)
        n_of_rank.append(cnt)
    n_rows, e1_rows, r2_rows, e2_rows = [], [], [], []
    for r in range(N_KEYS // SUBLANES):
        x1 = s1[SUBLANES * r:SUBLANES * (r + 1)]
        x2 = s2[SUBLANES * r:SUBLANES * (r + 1)]
        n = jnp.zeros((SUBLANES, t), F32)
        r2 = jnp.full((SUBLANES, t), float(PEER_TOPK), F32)
        for q in range(PEER_TOPK - 1, -1, -1):
            n = jnp.where(x1 >= a[q], n_of_rank[q], n)
            r2 = jnp.where(x2 >= b[q], float(q), r2)
        n_rows.append(n)
        r2_rows.append(r2)
        e1_rows.append(jnp.exp(x1 - a[0]) * inv_z)
        e2_rows.append(jnp.exp(x2 - b[0]))
    cat = lambda rows: jnp.concatenate(rows, axis=0)
    return cat(n_rows), cat(e1_rows), cat(r2_rows), cat(e2_rows)


def _route_kernel(x_ref, mod_ref, nw_ref, wf_ref, h_ref, n_ref, e1_ref, r2_ref, e2_ref, s_ref):
    h = _norm_mod(x_ref[...], nw_ref[...], mod_ref[3:4, :], mod_ref[4:5, :]).astype(BF16)
    h_ref[...] = h
    s_ref[...] = _dot_nt(wf_ref[...], h)
    t = s_ref.shape[1]

    def per_head(hd, carry):
        r1 = pl.multiple_of(hd * 2 * N_KEYS, 2 * N_KEYS)
        for l in range(t // ROUTE_LANES):
            ls = slice(l * ROUTE_LANES, (l + 1) * ROUTE_LANES)
            s1 = s_ref[pl.ds(r1, N_KEYS), ls]
            s2 = s_ref[pl.ds(r1 + N_KEYS, N_KEYS), ls]
            n, e1, r2, e2 = _route_tables(s1, s2)
            n_ref[hd, :, ls] = n
            e1_ref[hd, :, ls] = e1
            r2_ref[hd, :, ls] = r2.astype(BF16)
            e2_ref[hd, :, ls] = e2.astype(BF16)
        return carry

    lax.fori_loop(0, PEER_HEADS, per_head, 0)


def _dense_kernel(h_ref, u_ref, v_ref, n_ref, e1_ref, r2_ref, e2_ref, x_ref, mod_ref, o_ref,
                  acc_ref, a_ref, p_ref):
    c = pl.program_id(1)
    n_rows = u_ref.shape[0] // N_KEYS
    tb = h_ref.shape[0]
    sub = 16

    @pl.when(c == 0)
    def _():
        acc_ref[...] = jnp.zeros_like(acc_ref)

    a_ref[...] = _dot_nt(u_ref[...], h_ref[...])

    def body(l, carry):
        ls = pl.ds(pl.multiple_of(l * MASK_LANES, MASK_LANES), MASK_LANES)
        for ii in range(n_rows):
            rs = slice(ii * N_KEYS, (ii + 1) * N_KEYS)
            w = jnp.zeros((N_KEYS // sub, sub, MASK_LANES), BF16)
            for hd in range(PEER_HEADS):
                nb = jnp.broadcast_to(n_ref[hd, ii:ii + 1, ls], (sub, MASK_LANES)).astype(BF16)
                eb = jnp.broadcast_to(e1_ref[hd, ii:ii + 1, ls], (sub, MASK_LANES)).astype(BF16)
                r2 = r2_ref[hd, :, ls].reshape(N_KEYS // sub, sub, MASK_LANES)
                e2 = e2_ref[hd, :, ls].reshape(N_KEYS // sub, sub, MASK_LANES)
                w = w + jnp.where(r2 < nb[None], e2 * eb[None], jnp.zeros_like(e2))
            act = jax.nn.gelu(a_ref[rs, ls]).astype(BF16)
            p_ref[rs, ls] = act * w.reshape(N_KEYS, MASK_LANES)
        return carry

    lax.fori_loop(0, tb // MASK_LANES, body, 0)
    acc_ref[...] += _dot_tn(p_ref[...], v_ref[...])

    @pl.when(c == pl.num_programs(1) - 1)
    def _():
        o_ref[...] = x_ref[...] + mod_ref[5:6, :] * acc_ref[...]


def _final_norm_kernel(x_ref, nw_ref, o_ref):
    x = x_ref[...]
    ms = jnp.mean(x * x, axis=-1, keepdims=True)
    o_ref[...] = x * lax.rsqrt(ms + EPS) * nw_ref[...]


def _rope_tables(seq_len, n_identity):
    axis_dim = HEAD_DIM // 2
    n_rows = seq_len // GRID_W
    row = jnp.repeat(jnp.arange(n_rows, dtype=F32), GRID_W)
    col = jnp.tile(jnp.arange(GRID_W, dtype=F32), n_rows)
    inv = ROPE_THETA ** (-jnp.arange(0, axis_dim, 2, dtype=F32) / axis_dim)
    ang_r = row[:, None] * inv
    ang_c = col[:, None] * inv
    cos = jnp.concatenate([jnp.cos(ang_r)] * 2 + [jnp.cos(ang_c)] * 2, axis=-1)
    sin = jnp.concatenate([-jnp.sin(ang_r), jnp.sin(ang_r), -jnp.sin(ang_c), jnp.sin(ang_c)], axis=-1)
    cos = jnp.concatenate([jnp.ones((n_identity, HEAD_DIM), F32), cos], axis=0)
    sin = jnp.concatenate([jnp.zeros((n_identity, HEAD_DIM), F32), sin], axis=0)
    return jnp.tile(cos, (1, N_HEADS)), jnp.tile(sin, (1, N_HEADS))


def _head_selectors(n_heads):
    c = jnp.arange(n_heads * HEAD_DIM)[:, None] // HEAD_DIM
    sel = (c == jnp.arange(128)[None, :]).astype(BF16)
    return sel, sel.T


def _peer_route(x, mods, norm_ffn3, wf_all, layer, cond_row):
    tt, d = x.shape
    tb = TOKEN_BLOCK
    n_keys2 = PEER_HEADS * 2 * N_KEYS
    tab = lambda dtype: jax.ShapeDtypeStruct((PEER_HEADS, N_KEYS, tt), dtype)
    tab_spec = pl.BlockSpec((PEER_HEADS, N_KEYS, tb), lambda b: (0, 0, b))
    row_spec = pl.BlockSpec((tb, d), lambda b: (b, 0))
    h, *tabs = pl.pallas_call(
        _route_kernel,
        grid=(tt // tb,),
        in_specs=[row_spec,
                  pl.BlockSpec((None, None, N_MOD, d), lambda b: (layer, cond_row(b, tb), 0, 0)),
                  pl.BlockSpec((None, 1, d), lambda b: (layer, 0, 0)),
                  pl.BlockSpec((None, n_keys2, d), lambda b: (layer, 0, 0))],
        out_specs=[row_spec, tab_spec, tab_spec, tab_spec, tab_spec],
        out_shape=[jax.ShapeDtypeStruct((tt, d), BF16), tab(F32), tab(F32), tab(BF16), tab(BF16)],
        scratch_shapes=[pltpu.VMEM((n_keys2, tb), F32)],
        compiler_params=_cparams("parallel"),
        name="peer_route",
    )(x, mods, norm_ffn3, wf_all)
    return h, tabs


def _peer_dense(h, u_bf, v_bf, tabs, x, mods, layer, cond_row):
    tt, d = x.shape
    n_exp = u_bf.shape[1]
    dt, de = DENSE_TOKENS, DENSE_CHUNK
    n_tab, e1_tab, r2_tab, e2_tab = tabs
    dtab = pl.BlockSpec((PEER_HEADS, N_KEYS, dt), lambda b, e: (0, 0, b))
    rtab = pl.BlockSpec((PEER_HEADS, de // N_KEYS, dt), lambda b, e: (0, e, b))
    return pl.pallas_call(
        _dense_kernel,
        grid=(tt // dt, n_exp // de),
        in_specs=[
            pl.BlockSpec((dt, d), lambda b, e: (b, 0)),
            pl.BlockSpec((None, de, d), lambda b, e: (layer, e, 0)),
            pl.BlockSpec((None, de, d), lambda b, e: (layer, e, 0)),
            rtab, rtab, dtab, dtab,
            pl.BlockSpec((dt, d), lambda b, e: (b, 0)),
            pl.BlockSpec((None, None, N_MOD, d), lambda b, e: (layer, cond_row(b, dt), 0, 0)),
        ],
        out_specs=pl.BlockSpec((dt, d), lambda b, e: (b, 0)),
        out_shape=jax.ShapeDtypeStruct((tt, d), F32),
        scratch_shapes=[pltpu.VMEM((dt, d), F32), pltpu.VMEM((de, dt), F32), pltpu.VMEM((de, dt), BF16)],
        compiler_params=_cparams("parallel", "arbitrary"),
        name="peer_dense",
    )(h, u_bf, v_bf, n_tab, e1_tab, r2_tab, e2_tab, x, mods)


def kernel(x_prompt, x_sample, cache_k, cache_v, c, c_ctx, w_ada, b_ada, norm_mix, norm_ffn, norm_final,
           w_qkv, q_gain, k_gain, w_o, w_bcx, conv_w, conv_b, w_conv_out, w_pq, sub_keys, u_exp, v_exp):
    batch, seq, d = x_prompt.shape
    dec_batch, dec_seq, _ = x_sample.shape
    depth = w_ada.shape[0]
    past = cache_k.shape[2]
    tp = batch * seq
    ts = dec_batch * dec_seq
    tt = tp + ts
    tb = TOKEN_BLOCK
    assert seq % tb == 0 and dec_seq % tb == 0 and dec_batch + 1 <= N_COND_ROWS
    assert tp % DENSE_TOKENS == 0 and dec_seq % DENSE_TOKENS == 0
    assert d == N_HEADS * HEAD_DIM and seq == tb and past == tb and tp % dec_seq == 0
    npb = tp // tb
    bpp = seq // tb
    bps = dec_seq // tb
    nblk = tt // tb
    dq = N_HEADS * HEAD_DIM
    dk = N_KV_HEADS * HEAD_DIM
    n_exp = u_exp.shape[1]
    assert n_exp == N_KEYS * N_KEYS and n_exp % DENSE_CHUNK == 0

    def cond_row(b, rows=tb):
        t0 = b * rows
        return jnp.where(t0 < tp, 0, 1 + (t0 - tp) // dec_seq)

    def mod_spec(layer):
        return pl.BlockSpec((None, None, N_MOD, d), lambda b: (layer, cond_row(b), 0, 0))

    def row_spec(width, dtype_rows=tb):
        return pl.BlockSpec((dtype_rows, width), lambda b: (b, 0))

    def layer_vec(layer, width):
        return pl.BlockSpec((None, 1, width), lambda b: (layer, 0, 0))

    def const2(shape):
        return pl.BlockSpec(shape, lambda b: (0, 0))

    x = jnp.concatenate([x_prompt.reshape(tp, d), x_sample.reshape(ts, d)], axis=0)
    cond = jnp.zeros((N_COND_ROWS, d), F32).at[0].set(c_ctx).at[1:1 + dec_batch].set(c)
    mods = _modulation(cond, w_ada, b_ada).reshape(depth, N_COND_ROWS, N_MOD, d)

    wf_all = _fold_keys(w_pq, sub_keys)
    u_bf = u_exp.astype(BF16)
    v_bf = v_exp.astype(BF16)
    w_qkv_bf = w_qkv.astype(BF16)
    w_o_bf = w_o.astype(BF16)
    w_bcx_bf = w_bcx.astype(BF16)
    w_co_bf = w_conv_out.astype(BF16)
    norm_mix3 = norm_mix.reshape(depth, 1, d)
    norm_ffn3 = norm_ffn.reshape(depth, 1, d)

    cos_t, sin_t = _rope_tables(dec_seq, tb)
    selq, selq_t = _head_selectors(N_HEADS)
    selk, selk_t = _head_selectors(N_KV_HEADS)
    scale = HEAD_DIM ** -0.5
    qg_all = jnp.tile(q_gain, (1, N_HEADS)).reshape(-1, 1, dq) * scale
    kg_all = jnp.tile(k_gain, (1, N_KV_HEADS)).reshape(-1, 1, dk)
    ck_all = jnp.transpose(cache_k, (1, 0, 3, 2, 4)).astype(BF16)
    cv_all = jnp.transpose(cache_v, (1, 0, 3, 2, 4)).astype(BF16)

    def rope_block(b):
        return jnp.where(b < npb, 0, 1 + (b - npb) % bps)

    new_k, new_v = [], []
    for i in range(depth):
        j = i // 2
        if i % 2 == 0:
            q, k_hm, v_hm, kf, vf = pl.pallas_call(
                _qkv_kernel,
                grid=(nblk,),
                in_specs=[
                    row_spec(d), mod_spec(i), layer_vec(i, d),
                    pl.BlockSpec((None, d, dq + 2 * dk), lambda b, j=j: (j, 0, 0)),
                    pl.BlockSpec((None, 1, dq), lambda b, j=j: (j, 0, 0)),
                    pl.BlockSpec((None, 1, dk), lambda b, j=j: (j, 0, 0)),
                    const2((dq, 128)), const2((128, dq)), const2((dk, 128)), const2((128, dk)),
                    pl.BlockSpec((tb, dq), lambda b: (rope_block(b), 0)),
                    pl.BlockSpec((tb, dq), lambda b: (rope_block(b), 0)),
                ],
                out_specs=[
                    row_spec(dq),
                    pl.BlockSpec((N_KV_HEADS, tb, HEAD_DIM), lambda b: (0, b, 0)),
                    pl.BlockSpec((N_KV_HEADS, tb, HEAD_DIM), lambda b: (0, b, 0)),
                    row_spec(dk), row_spec(dk),
                ],
                out_shape=[
                    jax.ShapeDtypeStruct((tt, dq), BF16),
                    jax.ShapeDtypeStruct((N_KV_HEADS, tt, HEAD_DIM), BF16),
                    jax.ShapeDtypeStruct((N_KV_HEADS, tt, HEAD_DIM), BF16),
                    jax.ShapeDtypeStruct((tt, dk), F32),
                    jax.ShapeDtypeStruct((tt, dk), F32),
                ],
                compiler_params=_cparams("parallel"),
                name="qkv_project",
            )(x, mods, norm_mix3, w_qkv_bf, qg_all, kg_all, selq, selq_t, selk, selk_t, cos_t, sin_t)
            new_k.append(kf[:tp].reshape(batch, seq, N_KV_HEADS, HEAD_DIM))
            new_v.append(vf[:tp].reshape(batch, seq, N_KV_HEADS, HEAD_DIM))

            def sample_seq(u):
                return jnp.maximum(u - npb, 0) // bps

            kv_a = pl.BlockSpec((None, tb, HEAD_DIM), lambda u, g: (g, jnp.minimum(u, npb - 1), 0))
            kv_b = pl.BlockSpec((None, dec_seq, HEAD_DIM), lambda u, g: (g, tp // dec_seq + sample_seq(u), 0))
            kv_c = pl.BlockSpec((None, None, None, past, HEAD_DIM), lambda u, g, j=j: (j, sample_seq(u), g, 0, 0))
            attn = pl.pallas_call(
                functools.partial(_attn_kernel, n_prompt_blocks=npb),
                grid=(nblk, N_KV_HEADS),
                in_specs=[pl.BlockSpec((tb, dk), lambda u, g: (u, g)), kv_a, kv_a, kv_b, kv_b, kv_c, kv_c],
                out_specs=pl.BlockSpec((tb, dk), lambda u, g: (u, g)),
                out_shape=jax.ShapeDtypeStruct((tt, dq), BF16),
                compiler_params=_cparams("parallel", "parallel"),
                name="attention",
            )(q, k_hm, v_hm, k_hm, v_hm, ck_all, cv_all)

            x = pl.pallas_call(
                _proj_res_kernel,
                grid=(nblk,),
                in_specs=[row_spec(dq), pl.BlockSpec((None, dq, d), lambda b, j=j: (j, 0, 0)), row_spec(d), mod_spec(i)],
                out_specs=row_spec(d),
                out_shape=jax.ShapeDtypeStruct((tt, d), F32),
                compiler_params=_cparams("parallel"),
                name="attn_out_project",
            )(attn, w_o_bf, x, mods)
        else:
            u, gb = pl.pallas_call(
                _conv_in_kernel,
                grid=(nblk,),
                in_specs=[row_spec(d), mod_spec(i), layer_vec(i, d),
                          pl.BlockSpec((None, d, 3 * d), lambda b, j=j: (j, 0, 0))],
                out_specs=[row_spec(d), row_spec(d)],
                out_shape=[jax.ShapeDtypeStruct((tt, d), F32), jax.ShapeDtypeStruct((tt, d), BF16)],
                compiler_params=_cparams("parallel"),
                name="conv_in_project",
            )(x, mods, norm_mix3, w_bcx_bf)
            halo = tb // 8
            x = pl.pallas_call(
                functools.partial(_conv_out_kernel, n_prompt_blocks=npb, blocks_per_prompt_seq=bpp,
                                  blocks_per_sample_seq=bps),
                grid=(nblk,),
                in_specs=[
                    row_spec(d),
                    pl.BlockSpec((8, d), lambda b: (jnp.maximum(b * halo - 1, 0), 0)),
                    pl.BlockSpec((8, d), lambda b: (jnp.minimum((b + 1) * halo, nblk * halo - 1), 0)),
                    row_spec(d),
                    pl.BlockSpec((None, 3, d), lambda b, j=j: (j, 0, 0)),
                    pl.BlockSpec((None, 1, d), lambda b, j=j: (j, 0, 0)),
                    pl.BlockSpec((None, d, d), lambda b, j=j: (j, 0, 0)),
                    row_spec(d), mod_spec(i),
                ],
                out_specs=row_spec(d),
                out_shape=jax.ShapeDtypeStruct((tt, d), F32),
                compiler_params=_cparams("parallel"),
                name="conv_out_project",
            )(u, u, u, gb, conv_w, conv_b.reshape(-1, 1, d), w_co_bf, x, mods)

        h, tabs = _peer_route(x, mods, norm_ffn3, wf_all, i, cond_row)
        x = _peer_dense(h, u_bf, v_bf, tabs, x, mods, i, cond_row)

    y = pl.pallas_call(
        _final_norm_kernel,
        grid=(nblk,),
        in_specs=[row_spec(d), const2((1, d))],
        out_specs=row_spec(d),
        out_shape=jax.ShapeDtypeStruct((tt, d), F32),
        compiler_params=_cparams("parallel"),
        name="final_norm",
    )(x, norm_final.reshape(1, d))

    y_prompt = y[:tp].reshape(batch, seq, d)
    y_sample = y[tp:].reshape(dec_batch, dec_seq, d)
    return (y_prompt, y_sample, jnp.stack(new_k, axis=1), jnp.stack(new_v, axis=1))
```

```python
import functools

import jax
import jax.numpy as jnp
from jax import lax
from jax.experimental import pallas as pl
from jax.experimental.pallas import tpu as pltpu

F32 = jnp.float32
BF16 = jnp.bfloat16

GRID_W = 64
N_HEADS = 16
N_KV_HEADS = 4
HEAD_DIM = 64
ROT_HALF = 16
ROPE_THETA = 10000.0
PEER_HEADS = 8
N_KEYS = 128
PEER_TOPK = 16
N_MOD = 6
EPS = 1e-6
N_COND_ROWS = 8
SUBLANES = 8

TOKEN_BLOCK = 256
DENSE_TOKENS = 512
DENSE_CHUNK = 1024
MASK_LANES = 256
ROUTE_LANES = 128
VMEM_LIMIT = 48 * 1024 * 1024


def _cparams(*sem):
    return pltpu.CompilerParams(dimension_semantics=sem, vmem_limit_bytes=VMEM_LIMIT)


def _split_bf16(a):
    hi = a.astype(BF16)
    lo = (a - hi.astype(F32)).astype(BF16)
    return hi, lo


def _norm_mod(x, nw, shift, scale):
    ms = jnp.mean(x * x, axis=-1, keepdims=True)
    return (x * lax.rsqrt(ms + EPS)) * nw * (1.0 + scale) + shift


def _dot_nt(a, b):
    return lax.dot_general(a, b, (((1,), (1,)), ((), ())), preferred_element_type=F32)


def _dot_tn(a, b):
    return lax.dot_general(a, b, (((0,), (0,)), ((), ())), preferred_element_type=F32)


def _mod_kernel(c_ref, w_ref, b_ref, o_ref):
    c = c_ref[...]
    s_hi, s_lo = _split_bf16(jax.nn.silu(c))
    w_hi, w_lo = _split_bf16(w_ref[...])
    acc = jnp.dot(s_hi, w_hi, preferred_element_type=F32)
    acc += jnp.dot(s_hi, w_lo, preferred_element_type=F32)
    acc += jnp.dot(s_lo, w_hi, preferred_element_type=F32)
    o_ref[...] = acc + b_ref[...]


def _modulation(cond, w_ada, b_ada):
    depth, d, nd = w_ada.shape
    nb = 1536
    return pl.pallas_call(
        _mod_kernel,
        grid=(depth, nd // nb),
        in_specs=[
            pl.BlockSpec((N_COND_ROWS, d), lambda l, n: (0, 0)),
            pl.BlockSpec((None, d, nb), lambda l, n: (l, 0, n)),
            pl.BlockSpec((None, 1, nb), lambda l, n: (l, 0, n)),
        ],
        out_specs=pl.BlockSpec((None, N_COND_ROWS, nb), lambda l, n: (l, 0, n)),
        out_shape=jax.ShapeDtypeStruct((depth, N_COND_ROWS, nd), F32),
        compiler_params=_cparams("parallel", "parallel"),
        name="adaln_modulation",
    )(cond, w_ada, b_ada.reshape(depth, 1, nd))


def _fold_kernel(sk_ref, w_ref, o_ref):
    k_hi, k_lo = _split_bf16(sk_ref[...])
    w_hi, w_lo = _split_bf16(w_ref[...])
    acc = _dot_nt(k_hi, w_hi) + _dot_nt(k_hi, w_lo) + _dot_nt(k_lo, w_hi)
    o_ref[...] = acc.astype(BF16)


def _fold_keys(w_pq, sub_keys):
    depth, d, _ = w_pq.shape
    dkh = sub_keys.shape[-1]
    n_parts = PEER_HEADS * 2
    return pl.pallas_call(
        _fold_kernel,
        grid=(depth, n_parts),
        in_specs=[
            pl.BlockSpec((None, None, None, N_KEYS, dkh), lambda l, p: (l, p // 2, p % 2, 0, 0)),
            pl.BlockSpec((None, d, dkh), lambda l, p: (l, 0, p)),
        ],
        out_specs=pl.BlockSpec((None, N_KEYS, d), lambda l, p: (l, p, 0)),
        out_shape=jax.ShapeDtypeStruct((depth, n_parts * N_KEYS, d), BF16),
        compiler_params=_cparams("parallel", "parallel"),
        name="peer_fold_keys",
    )(sub_keys, w_pq)


def _head_rms_scale(q, sel, sel_t):
    hi, lo = _split_bf16(q * q)
    ss = jnp.dot(hi, sel, preferred_element_type=F32) + jnp.dot(lo, sel, preferred_element_type=F32)
    r = lax.rsqrt(ss * (1.0 / HEAD_DIM) + EPS)
    r_hi, r_lo = _split_bf16(r)
    return jnp.dot(r_hi, sel_t, preferred_element_type=F32) + jnp.dot(r_lo, sel_t, preferred_element_type=F32)


def _rope(x, cos, sin_signed):
    n = x.shape[1]
    lane = lax.broadcasted_iota(jnp.int32, x.shape, 1)
    upper = (lane & ROT_HALF) != 0
    partner = jnp.where(upper, pltpu.roll(x, ROT_HALF, axis=1), pltpu.roll(x, n - ROT_HALF, axis=1))
    return x * cos + partner * sin_signed


def _qkv_kernel(x_ref, mod_ref, nw_ref, w_ref, qg_ref, kg_ref, selq_ref, selqt_ref, selk_ref, selkt_ref,
                cos_ref, sin_ref, q_ref, k_ref, v_ref, kf_ref, vf_ref):
    dq = N_HEADS * HEAD_DIM
    dk = N_KV_HEADS * HEAD_DIM
    h = _norm_mod(x_ref[...], nw_ref[...], mod_ref[0:1, :], mod_ref[1:2, :]).astype(BF16)
    qkv = jnp.dot(h, w_ref[...], preferred_element_type=F32)
    q = qkv[:, :dq]
    k = qkv[:, dq:dq + dk]
    v = qkv[:, dq + dk:]
    cos = cos_ref[...]
    sin = sin_ref[...]
    qn = q * _head_rms_scale(q, selq_ref[...], selqt_ref[...]) * qg_ref[...]
    q_ref[...] = _rope(qn, cos, sin).astype(BF16)
    kn = k * _head_rms_scale(k, selk_ref[...], selkt_ref[...]) * kg_ref[...]
    kf_ref[...] = kn
    vf_ref[...] = v
    kr = _rope(kn, cos[:, :dk], sin[:, :dk]).astype(BF16)
    vb = v.astype(BF16)
    for g in range(N_KV_HEADS):
        k_ref[g] = kr[:, g * HEAD_DIM:(g + 1) * HEAD_DIM]
        v_ref[g] = vb[:, g * HEAD_DIM:(g + 1) * HEAD_DIM]


def _softmax_pv(qh, parts):
    scores = [_dot_nt(qh, k) for k, _ in parts]
    m = scores[0].max(axis=-1, keepdims=True)
    for s in scores[1:]:
        m = jnp.maximum(m, s.max(axis=-1, keepdims=True))
    num = None
    den = None
    for s, (_, v) in zip(scores, parts):
        p = jnp.exp(s - m)
        d = p.sum(axis=-1, keepdims=True)
        o = jnp.dot(p.astype(BF16), v, preferred_element_type=F32)
        num = o if num is None else num + o
        den = d if den is None else den + d
    return num / den


def _attn_kernel(q_ref, ka_ref, va_ref, kb_ref, vb_ref, kc_ref, vc_ref, o_ref, *, n_prompt_blocks):
    u = pl.program_id(0)
    group = N_HEADS // N_KV_HEADS

    @pl.when(u < n_prompt_blocks)
    def _():
        parts = [(ka_ref[...], va_ref[...])]
        for hh in range(group):
            sl = slice(hh * HEAD_DIM, (hh + 1) * HEAD_DIM)
            o_ref[:, sl] = _softmax_pv(q_ref[:, sl], parts).astype(BF16)

    @pl.when(u >= n_prompt_blocks)
    def _():
        parts = [(kb_ref[...], vb_ref[...]), (kc_ref[...], vc_ref[...])]
        for hh in range(group):
            sl = slice(hh * HEAD_DIM, (hh + 1) * HEAD_DIM)
            o_ref[:, sl] = _softmax_pv(q_ref[:, sl], parts).astype(BF16)


def _proj_res_kernel(a_ref, w_ref, x_ref, mod_ref, o_ref):
    o_ref[...] = x_ref[...] + mod_ref[2:3, :] * jnp.dot(a_ref[...], w_ref[...], preferred_element_type=F32)


def _conv_in_kernel(x_ref, mod_ref, nw_ref, w_ref, u_ref, gb_ref):
    d = x_ref.shape[1]
    h = _norm_mod(x_ref[...], nw_ref[...], mod_ref[0:1, :], mod_ref[1:2, :]).astype(BF16)
    bcx = jnp.dot(h, w_ref[...], preferred_element_type=F32)
    gb_ref[...] = bcx[:, :d].astype(BF16)
    u_ref[...] = bcx[:, d:2 * d] * bcx[:, 2 * d:]


def _conv_out_kernel(u_ref, up_ref, un_ref, gb_ref, cw_ref, cb_ref, w_ref, x_ref, mod_ref, o_ref, *,
                     n_prompt_blocks, blocks_per_prompt_seq, blocks_per_sample_seq):
    tb = u_ref.shape[0]
    b = pl.program_id(0)
    bs = b - n_prompt_blocks
    pos = jnp.where(b < n_prompt_blocks, b % blocks_per_prompt_seq, bs % blocks_per_sample_seq)
    per_seq = jnp.where(b < n_prompt_blocks, blocks_per_prompt_seq, blocks_per_sample_seq)
    u = u_ref[...]
    prev_row = jnp.where(pos == 0, 0.0, up_ref[7:8, :])
    next_row = jnp.where(pos == per_seq - 1, 0.0, un_ref[0:1, :])
    row = lax.broadcasted_iota(jnp.int32, u.shape, 0)
    u_m1 = jnp.where(row == 0, prev_row, pltpu.roll(u, 1, axis=0))
    u_p1 = jnp.where(row == tb - 1, next_row, pltpu.roll(u, tb - 1, axis=0))
    conv = u_m1 * cw_ref[0:1, :] + u * cw_ref[1:2, :] + u_p1 * cw_ref[2:3, :] + cb_ref[...]
    y = (gb_ref[...].astype(F32) * conv).astype(BF16)
    o_ref[...] = x_ref[...] + mod_ref[2:3, :] * jnp.dot(y, w_ref[...], preferred_element_type=F32)


def _oddeven_merge(lo, hi, r):
    step = r * 2
    if step < hi - lo:
        yield from _oddeven_merge(lo, hi, step)
        yield from _oddeven_merge(lo + r, hi, step)
        yield from [(i, i + r) for i in range(lo + r, hi - r, step)]
    else:
        yield (lo, lo + r)


def _oddeven_sort(lo, hi):
    if hi - lo >= 1:
        mid = lo + (hi - lo) // 2
        yield from _oddeven_sort(lo, mid)
        yield from _oddeven_sort(mid + 1, hi)
        yield from _oddeven_merge(lo, hi, 1)


def _bitonic_merge_pairs(n):
    out, s = [], n // 2
    while s >= 1:
        out += [(i, i + s) for i in range(n) if (i // s) % 2 == 0]
        s //= 2
    return out


SORT16 = tuple(_oddeven_sort(0, 15))
SORT8 = tuple(_oddeven_sort(0, 7))
BITONIC16 = tuple(_bitonic_merge_pairs(16))


def _apply_net(net, v):
    v = list(v)
    for i, j in net:
        hi, lo = jnp.maximum(v[i], v[j]), jnp.minimum(v[i], v[j])
        v[i], v[j] = hi, lo
    return v


def _merge_top16(a, b):
    return _apply_net(BITONIC16, [jnp.maximum(a[i], b[PEER_TOPK - 1 - i]) for i in range(PEER_TOPK)])


def _sorted_top16(x):
    v = _apply_net(SORT16, [x[SUBLANES * r:SUBLANES * (r + 1)] for r in range(N_KEYS // SUBLANES)])
    for shift in (4, 2, 1):
        v = _merge_top16(v, [pltpu.roll(u, shift, axis=0) for u in v])
    return v


def _route_tables(s1, s2):
    t = s1.shape[1]
    a = _sorted_top16(s1)
    b = _sorted_top16(s2)
    neg = jnp.full((SUBLANES, t), -jnp.inf, F32)
    c = {(p, q): a[p - 1] + b[q - 1] for p in range(1, PEER_TOPK + 1) for q in range(1, PEER_TOPK // p + 1)}
    row1 = [c[(1, q)] for q in range(1, 17)]
    col1 = [c[(p, 1)] for p in range(2, 17)] + [neg]
    row2 = [c[(2, q)] for q in range(2, 9)] + [neg]
    col2 = [c[(p, 2)] for p in range(3, 9)] + [neg, neg]
    rest = [c[(3, 3)], c[(3, 4)], c[(3, 5)], c[(4, 3)], c[(5, 3)], c[(4, 4)], neg, neg]
    m2 = _apply_net(BITONIC16, row2 + col2[::-1])
    m3 = _apply_net(SORT8, rest) + [neg] * 8
    top = _merge_top16(_merge_top16(row1, col1), _merge_top16(m2, m3))
    tau = top[PEER_TOPK - 1]
    z = jnp.zeros((SUBLANES, t), F32)
    for v in top:
        z = z + jnp.exp(v - top[0])
    inv_z = 1.0 / z
    n_of_rank = []
    for p in range(1, PEER_TOPK + 1):
        cnt = jnp.zeros((SUBLANES, t), F32)
        for q in range(1, PEER_TOPK // p + 1):
            cnt = cnt + jnp.where(c[(p, q)] >= tau, 1.0, 0.0)
        n_of_rank.append(cnt)
    n_rows, e1_rows, r2_rows, e2_rows = [], [], [], []
    for r in range(N_KEYS // SUBLANES):
        x1 = s1[SUBLANES * r:SUBLANES * (r + 1)]
        x2 = s2[SUBLANES * r:SUBLANES * (r + 1)]
        n = jnp.zeros((SUBLANES, t), F32)
        r2 = jnp.full((SUBLANES, t), float(PEER_TOPK), F32)
        for q in range(PEER_TOPK - 1, -1, -1):
            n = jnp.where(x1 >= a[q], n_of_rank[q], n)
            r2 = jnp.where(x2 >= b[q], float(q), r2)
        n_rows.append(n)
        r2_rows.append(r2)
        e1_rows.append(jnp.exp(x1 - a[0]) * inv_z)
        e2_rows.append(jnp.exp(x2 - b[0]))
    cat = lambda rows: jnp.concatenate(rows, axis=0)
    return cat(n_rows), cat(e1_rows), cat(r2_rows), cat(e2_rows)


def _route_kernel(x_ref, mod_ref, nw_ref, wf_ref, h_ref, n_ref, e1_ref, r2_ref, e2_ref, s_ref):
    h = _norm_mod(x_ref[...], nw_ref[...], mod_ref[3:4, :], mod_ref[4:5, :]).astype(BF16)
    h_ref[...] = h
    s_ref[...] = _dot_nt(wf_ref[...], h)
    t = s_ref.shape[1]

    def per_head(hd, carry):
        r1 = pl.multiple_of(hd * 2 * N_KEYS, 2 * N_KEYS)
        for l in range(t // ROUTE_LANES):
            ls = slice(l * ROUTE_LANES, (l + 1) * ROUTE_LANES)
            s1 = s_ref[pl.ds(r1, N_KEYS), ls]
            s2 = s_ref[pl.ds(r1 + N_KEYS, N_KEYS), ls]
            n, e1, r2, e2 = _route_tables(s1, s2)
            n_ref[hd, :, ls] = n
            e1_ref[hd, :, ls] = e1
            r2_ref[hd, :, ls] = r2.astype(BF16)
            e2_ref[hd, :, ls] = e2.astype(BF16)
        return carry

    lax.fori_loop(0, PEER_HEADS, per_head, 0)


def _dense_kernel(h_ref, u_ref, v_ref, n_ref, e1_ref, r2_ref, e2_ref, x_ref, mod_ref, o_ref,
                  acc_ref, a_ref, p_ref):
    c = pl.program_id(1)
    n_rows = u_ref.shape[0] // N_KEYS
    tb = h_ref.shape[0]
    sub = SUBLANES

    @pl.when(c == 0)
    def _():
        acc_ref[...] = jnp.zeros_like(acc_ref)

    a_ref[...] = _dot_nt(u_ref[...], h_ref[...])

    def body(l, carry):
        ls = pl.ds(pl.multiple_of(l * MASK_LANES, MASK_LANES), MASK_LANES)
        for ii in range(n_rows):
            rs = slice(ii * N_KEYS, (ii + 1) * N_KEYS)
            w = jnp.zeros((N_KEYS // sub, sub, MASK_LANES), BF16)
            for hd in range(PEER_HEADS):
                nb = jnp.broadcast_to(n_ref[hd, ii:ii + 1, ls], (sub, MASK_LANES)).astype(BF16)
                eb = jnp.broadcast_to(e1_ref[hd, ii:ii + 1, ls], (sub, MASK_LANES)).astype(BF16)
                r2 = r2_ref[hd, :, ls].reshape(N_KEYS // sub, sub, MASK_LANES)
                e2 = e2_ref[hd, :, ls].reshape(N_KEYS // sub, sub, MASK_LANES)
                w = w + jnp.where(r2 < nb[None], e2 * eb[None], jnp.zeros_like(e2))
            act = jax.nn.gelu(a_ref[rs, ls]).astype(BF16)
            p_ref[rs, ls] = act * w.reshape(N_KEYS, MASK_LANES)
        return carry

    lax.fori_loop(0, tb // MASK_LANES, body, 0)
    acc_ref[...] += _dot_tn(p_ref[...], v_ref[...])

    @pl.when(c == pl.num_programs(1) - 1)
    def _():
        o_ref[...] = x_ref[...] + mod_ref[5:6, :] * acc_ref[...]


def _final_norm_kernel(x_ref, nw_ref, o_ref):
    x = x_ref[...]
    ms = jnp.mean(x * x, axis=-1, keepdims=True)
    o_ref[...] = x * lax.rsqrt(ms + EPS) * nw_ref[...]


def _rope_tables(seq_len, n_identity):
    axis_dim = HEAD_DIM // 2
    n_rows = seq_len // GRID_W
    row = jnp.repeat(jnp.arange(n_rows, dtype=F32), GRID_W)
    col = jnp.tile(jnp.arange(GRID_W, dtype=F32), n_rows)
    inv = ROPE_THETA ** (-jnp.arange(0, axis_dim, 2, dtype=F32) / axis_dim)
    ang_r = row[:, None] * inv
    ang_c = col[:, None] * inv
    cos = jnp.concatenate([jnp.cos(ang_r)] * 2 + [jnp.cos(ang_c)] * 2, axis=-1)
    sin = jnp.concatenate([-jnp.sin(ang_r), jnp.sin(ang_r), -jnp.sin(ang_c), jnp.sin(ang_c)], axis=-1)
    cos = jnp.concatenate([jnp.ones((n_identity, HEAD_DIM), F32), cos], axis=0)
    sin = jnp.concatenate([jnp.zeros((n_identity, HEAD_DIM), F32), sin], axis=0)
    return jnp.tile(cos, (1, N_HEADS)), jnp.tile(sin, (1, N_HEADS))


def _head_selectors(n_heads):
    c = jnp.arange(n_heads * HEAD_DIM)[:, None] // HEAD_DIM
    sel = (c == jnp.arange(128)[None, :]).astype(BF16)
    return sel, sel.T


def _peer_route(x, mods, norm_ffn3, wf_all, layer, cond_row):
    tt, d = x.shape
    tb = TOKEN_BLOCK
    n_keys2 = PEER_HEADS * 2 * N_KEYS
    tab = lambda dtype: jax.ShapeDtypeStruct((PEER_HEADS, N_KEYS, tt), dtype)
    tab_spec = pl.BlockSpec((PEER_HEADS, N_KEYS, tb), lambda b: (0, 0, b))
    row_spec = pl.BlockSpec((tb, d), lambda b: (b, 0))
    h, *tabs = pl.pallas_call(
        _route_kernel,
        grid=(tt // tb,),
        in_specs=[row_spec,
                  pl.BlockSpec((None, None, N_MOD, d), lambda b: (layer, cond_row(b, tb), 0, 0)),
                  pl.BlockSpec((None, 1, d), lambda b: (layer, 0, 0)),
                  pl.BlockSpec((None, n_keys2, d), lambda b: (layer, 0, 0))],
        out_specs=[row_spec, tab_spec, tab_spec, tab_spec, tab_spec],
        out_shape=[jax.ShapeDtypeStruct((tt, d), BF16), tab(F32), tab(F32), tab(BF16), tab(BF16)],
        scratch_shapes=[pltpu.VMEM((n_keys2, tb), F32)],
        compiler_params=_cparams("parallel"),
        name="peer_route",
    )(x, mods, norm_ffn3, wf_all)
    return h, tabs


def _peer_dense(h, u_bf, v_bf, tabs, x, mods, layer, cond_row):
    tt, d = x.shape
    n_exp = u_bf.shape[1]
    dt, de = DENSE_TOKENS, DENSE_CHUNK
    n_tab, e1_tab, r2_tab, e2_tab = tabs
    dtab = pl.BlockSpec((PEER_HEADS, N_KEYS, dt), lambda b, e: (0, 0, b))
    rtab = pl.BlockSpec((PEER_HEADS, de // N_KEYS, dt), lambda b, e: (0, e, b))
    return pl.pallas_call(
        _dense_kernel,
        grid=(tt // dt, n_exp // de),
        in_specs=[
            pl.BlockSpec((dt, d), lambda b, e: (b, 0)),
            pl.BlockSpec((None, de, d), lambda b, e: (layer, e, 0)),
            pl.BlockSpec((None, de, d), lambda b, e: (layer, e, 0)),
            rtab, rtab, dtab, dtab,
            pl.BlockSpec((dt, d), lambda b, e: (b, 0)),
            pl.BlockSpec((None, None, N_MOD, d), lambda b, e: (layer, cond_row(b, dt), 0, 0)),
        ],
        out_specs=pl.BlockSpec((dt, d), lambda b, e: (b, 0)),
        out_shape=jax.ShapeDtypeStruct((tt, d), F32),
        scratch_shapes=[pltpu.VMEM((dt, d), F32), pltpu.VMEM((de, dt), F32), pltpu.VMEM((de, dt), BF16)],
        compiler_params=_cparams("parallel", "arbitrary"),
        name="peer_dense",
    )(h, u_bf, v_bf, n_tab, e1_tab, r2_tab, e2_tab, x, mods)


def kernel(x_prompt, x_sample, cache_k, cache_v, c, c_ctx, w_ada, b_ada, norm_mix, norm_ffn, norm_final,
           w_qkv, q_gain, k_gain, w_o, w_bcx, conv_w, conv_b, w_conv_out, w_pq, sub_keys, u_exp, v_exp):
    batch, seq, d = x_prompt.shape
    dec_batch, dec_seq, _ = x_sample.shape
    depth = w_ada.shape[0]
    past = cache_k.shape[2]
    tp = batch * seq
    ts = dec_batch * dec_seq
    tt = tp + ts
    tb = TOKEN_BLOCK
    assert seq % tb == 0 and dec_seq % tb == 0 and dec_batch + 1 <= N_COND_ROWS
    assert tp % DENSE_TOKENS == 0 and dec_seq % DENSE_TOKENS == 0
    assert d == N_HEADS * HEAD_DIM and seq == tb and past == tb and tp % dec_seq == 0
    npb = tp // tb
    bpp = seq // tb
    bps = dec_seq // tb
    nblk = tt // tb
    dq = N_HEADS * HEAD_DIM
    dk = N_KV_HEADS * HEAD_DIM
    n_exp = u_exp.shape[1]
    assert n_exp == N_KEYS * N_KEYS and n_exp % DENSE_CHUNK == 0

    def cond_row(b, rows=tb):
        t0 = b * rows
        return jnp.where(t0 < tp, 0, 1 + (t0 - tp) // dec_seq)

    def mod_spec(layer):
        return pl.BlockSpec((None, None, N_MOD, d), lambda b: (layer, cond_row(b), 0, 0))

    def row_spec(width, dtype_rows=tb):
        return pl.BlockSpec((dtype_rows, width), lambda b: (b, 0))

    def layer_vec(layer, width):
        return pl.BlockSpec((None, 1, width), lambda b: (layer, 0, 0))

    def const2(shape):
        return pl.BlockSpec(shape, lambda b: (0, 0))

    x = jnp.concatenate([x_prompt.reshape(tp, d), x_sample.reshape(ts, d)], axis=0)
    cond = jnp.zeros((N_COND_ROWS, d), F32).at[0].set(c_ctx).at[1:1 + dec_batch].set(c)
    mods = _modulation(cond, w_ada, b_ada).reshape(depth, N_COND_ROWS, N_MOD, d)

    wf_all = _fold_keys(w_pq, sub_keys)
    u_bf = u_exp.astype(BF16)
    v_bf = v_exp.astype(BF16)
    w_qkv_bf = w_qkv.astype(BF16)
    w_o_bf = w_o.astype(BF16)
    w_bcx_bf = w_bcx.astype(BF16)
    w_co_bf = w_conv_out.astype(BF16)
    norm_mix3 = norm_mix.reshape(depth, 1, d)
    norm_ffn3 = norm_ffn.reshape(depth, 1, d)

    cos_t, sin_t = _rope_tables(dec_seq, tb)
    selq, selq_t = _head_selectors(N_HEADS)
    selk, selk_t = _head_selectors(N_KV_HEADS)
    scale = HEAD_DIM ** -0.5
    qg_all = jnp.tile(q_gain, (1, N_HEADS)).reshape(-1, 1, dq) * scale
    kg_all = jnp.tile(k_gain, (1, N_KV_HEADS)).reshape(-1, 1, dk)
    ck_all = jnp.transpose(cache_k, (1, 0, 3, 2, 4)).astype(BF16)
    cv_all = jnp.transpose(cache_v, (1, 0, 3, 2, 4)).astype(BF16)

    def rope_block(b):
        return jnp.where(b < npb, 0, 1 + (b - npb) % bps)

    new_k, new_v = [], []
    for i in range(depth):
        j = i // 2
        if i % 2 == 0:
            q, k_hm, v_hm, kf, vf = pl.pallas_call(
                _qkv_kernel,
                grid=(nblk,),
                in_specs=[
                    row_spec(d), mod_spec(i), layer_vec(i, d),
                    pl.BlockSpec((None, d, dq + 2 * dk), lambda b, j=j: (j, 0, 0)),
                    pl.BlockSpec((None, 1, dq), lambda b, j=j: (j, 0, 0)),
                    pl.BlockSpec((None, 1, dk), lambda b, j=j: (j, 0, 0)),
                    const2((dq, 128)), const2((128, dq)), const2((dk, 128)), const2((128, dk)),
                    pl.BlockSpec((tb, dq), lambda b: (rope_block(b), 0)),
                    pl.BlockSpec((tb, dq), lambda b: (rope_block(b), 0)),
                ],
                out_specs=[
                    row_spec(dq),
                    pl.BlockSpec((N_KV_HEADS, tb, HEAD_DIM), lambda b: (0, b, 0)),
                    pl.BlockSpec((N_KV_HEADS, tb, HEAD_DIM), lambda b: (0, b, 0)),
                    row_spec(dk), row_spec(dk),
                ],
                out_shape=[
                    jax.ShapeDtypeStruct((tt, dq), BF16),
                    jax.ShapeDtypeStruct((N_KV_HEADS, tt, HEAD_DIM), BF16),
                    jax.ShapeDtypeStruct((N_KV_HEADS, tt, HEAD_DIM), BF16),
                    jax.ShapeDtypeStruct((tt, dk), F32),
                    jax.ShapeDtypeStruct((tt, dk), F32),
                ],
                compiler_params=_cparams("parallel"),
                name="qkv_project",
            )(x, mods, norm_mix3, w_qkv_bf, qg_all, kg_all, selq, selq_t, selk, selk_t, cos_t, sin_t)
            new_k.append(kf[:tp].reshape(batch, seq, N_KV_HEADS, HEAD_DIM))
            new_v.append(vf[:tp].reshape(batch, seq, N_KV_HEADS, HEAD_DIM))

            def sample_seq(u):
                return jnp.maximum(u - npb, 0) // bps

            kv_a = pl.BlockSpec((None, tb, HEAD_DIM), lambda u, g: (g, jnp.minimum(u, npb - 1), 0))
            kv_b = pl.BlockSpec((None, dec_seq, HEAD_DIM), lambda u, g: (g, tp // dec_seq + sample_seq(u), 0))
            kv_c = pl.BlockSpec((None, None, None, past, HEAD_DIM), lambda u, g, j=j: (j, sample_seq(u), g, 0, 0))
            attn = pl.pallas_call(
                functools.partial(_attn_kernel, n_prompt_blocks=npb),
                grid=(nblk, N_KV_HEADS),
                in_specs=[pl.BlockSpec((tb, dk), lambda u, g: (u, g)), kv_a, kv_a, kv_b, kv_b, kv_c, kv_c],
                out_specs=pl.BlockSpec((tb, dk), lambda u, g: (u, g)),
                out_shape=jax.ShapeDtypeStruct((tt, dq), BF16),
                compiler_params=_cparams("parallel", "parallel"),
                name="attention",
            )(q, k_hm, v_hm, k_hm, v_hm, ck_all, cv_all)

            x = pl.pallas_call(
                _proj_res_kernel,
                grid=(nblk,),
                in_specs=[row_spec(dq), pl.BlockSpec((None, dq, d), lambda b, j=j: (j, 0, 0)), row_spec(d), mod_spec(i)],
                out_specs=row_spec(d),
                out_shape=jax.ShapeDtypeStruct((tt, d), F32),
                compiler_params=_cparams("parallel"),
                name="attn_out_project",
            )(attn, w_o_bf, x, mods)
        else:
            u, gb = pl.pallas_call(
                _conv_in_kernel,
                grid=(nblk,),
                in_specs=[row_spec(d), mod_spec(i), layer_vec(i, d),
                          pl.BlockSpec((None, d, 3 * d), lambda b, j=j: (j, 0, 0))],
                out_specs=[row_spec(d), row_spec(d)],
                out_shape=[jax.ShapeDtypeStruct((tt, d), F32), jax.ShapeDtypeStruct((tt, d), BF16)],
                compiler_params=_cparams("parallel"),
                name="conv_in_project",
            )(x, mods, norm_mix3, w_bcx_bf)
            halo = tb // 8
            x = pl.pallas_call(
                functools.partial(_conv_out_kernel, n_prompt_blocks=npb, blocks_per_prompt_seq=bpp,
                                  blocks_per_sample_seq=bps),
                grid=(nblk,),
                in_specs=[
                    row_spec(d),
                    pl.BlockSpec((8, d), lambda b: (jnp.maximum(b * halo - 1, 0), 0)),
                    pl.BlockSpec((8, d), lambda b: (jnp.minimum((b + 1) * halo, nblk * halo - 1), 0)),
                    row_spec(d),
                    pl.BlockSpec((None, 3, d), lambda b, j=j: (j, 0, 0)),
                    pl.BlockSpec((None, 1, d), lambda b, j=j: (j, 0, 0)),
                    pl.BlockSpec((None, d, d), lambda b, j=j: (j, 0, 0)),
                    row_spec(d), mod_spec(i),
                ],
                out_specs=row_spec(d),
                out_shape=jax.ShapeDtypeStruct((tt, d), F32),
                compiler_params=_cparams("parallel"),
                name="conv_out_project",
            )(u, u, u, gb, conv_w, conv_b.reshape(-1, 1, d), w_co_bf, x, mods)

        h, tabs = _peer_route(x, mods, norm_ffn3, wf_all, i, cond_row)
        x = _peer_dense(h, u_bf, v_bf, tabs, x, mods, i, cond_row)

    y = pl.pallas_call(
        _final_norm_kernel,
        grid=(nblk,),
        in_specs=[row_spec(d), const2((1, d))],
        out_specs=row_spec(d),
        out_shape=jax.ShapeDtypeStruct((tt, d), F32),
        compiler_params=_cparams("parallel"),
        name="final_norm",
    )(x, norm_final.reshape(1, d))

    y_prompt = y[:tp].reshape(batch, seq, d)
    y_sample = y[tp:].reshape(dec_batch, dec_seq, d)
    return (y_prompt, y_sample, jnp.stack(new_k, axis=1), jnp.stack(new_v, axis=1))
```

```python
import functools
import math

import jax
import jax.numpy as jnp
from jax import lax
from jax.experimental import pallas as pl
from jax.experimental.pallas import tpu as pltpu

F32 = jnp.float32
BF16 = jnp.bfloat16

GRID_W = 64
N_HEADS = 16
N_KV_HEADS = 4
HEAD_DIM = 64
ROT_HALF = 16
ROPE_THETA = 10000.0
PEER_HEADS = 8
N_KEYS = 128
PEER_TOPK = 16
N_MOD = 6
EPS = 1e-6
N_COND_ROWS = 8
SUBLANES = 8

TOKEN_BLOCK = 256
DENSE_TOKENS = 512
DENSE_CHUNK = 2048
MASK_LANES = 256
ROUTE_LANES = 128
VMEM_LIMIT = 52 * 1024 * 1024


def _cparams(*sem):
    return pltpu.CompilerParams(dimension_semantics=sem, vmem_limit_bytes=VMEM_LIMIT)


def _split_bf16(a):
    hi = a.astype(BF16)
    lo = (a - hi.astype(F32)).astype(BF16)
    return hi, lo


def _norm_mod(x, nw, shift, scale):
    ms = jnp.mean(x * x, axis=-1, keepdims=True)
    return (x * lax.rsqrt(ms + EPS)) * nw * (1.0 + scale) + shift


def _gelu_tanh(x):
    c0 = math.sqrt(2.0 / math.pi)
    inner = x * (c0 + (c0 * 0.044715) * (x * x))
    return (0.5 * x) * (1.0 + jnp.tanh(inner))


def _dot_nt(a, b):
    return lax.dot_general(a, b, (((1,), (1,)), ((), ())), preferred_element_type=F32)


def _dot_tn(a, b):
    return lax.dot_general(a, b, (((0,), (0,)), ((), ())), preferred_element_type=F32)


def _mod_kernel(c_ref, w_ref, b_ref, o_ref):
    c = c_ref[...]
    s_hi, s_lo = _split_bf16(jax.nn.silu(c))
    w_hi, w_lo = _split_bf16(w_ref[...])
    acc = jnp.dot(s_hi, w_hi, preferred_element_type=F32)
    acc += jnp.dot(s_hi, w_lo, preferred_element_type=F32)
    acc += jnp.dot(s_lo, w_hi, preferred_element_type=F32)
    o_ref[...] = acc + b_ref[...]


def _modulation(cond, w_ada, b_ada):
    depth, d, nd = w_ada.shape
    nb = 1536
    return pl.pallas_call(
        _mod_kernel,
        grid=(depth, nd // nb),
        in_specs=[
            pl.BlockSpec((N_COND_ROWS, d), lambda l, n: (0, 0)),
            pl.BlockSpec((None, d, nb), lambda l, n: (l, 0, n)),
            pl.BlockSpec((None, 1, nb), lambda l, n: (l, 0, n)),
        ],
        out_specs=pl.BlockSpec((None, N_COND_ROWS, nb), lambda l, n: (l, 0, n)),
        out_shape=jax.ShapeDtypeStruct((depth, N_COND_ROWS, nd), F32),
        compiler_params=_cparams("parallel", "parallel"),
        name="adaln_modulation",
    )(cond, w_ada, b_ada.reshape(depth, 1, nd))


def _fold_kernel(sk_ref, w_ref, o_ref):
    k_hi, k_lo = _split_bf16(sk_ref[...])
    w_hi, w_lo = _split_bf16(w_ref[...])
    acc = _dot_nt(k_hi, w_hi) + _dot_nt(k_hi, w_lo) + _dot_nt(k_lo, w_hi)
    o_ref[...] = acc.astype(BF16)


def _fold_keys(w_pq, sub_keys):
    depth, d, _ = w_pq.shape
    dkh = sub_keys.shape[-1]
    n_parts = PEER_HEADS * 2
    return pl.pallas_call(
        _fold_kernel,
        grid=(depth, n_parts),
        in_specs=[
            pl.BlockSpec((None, None, None, N_KEYS, dkh), lambda l, p: (l, p // 2, p % 2, 0, 0)),
            pl.BlockSpec((None, d, dkh), lambda l, p: (l, 0, p)),
        ],
        out_specs=pl.BlockSpec((None, N_KEYS, d), lambda l, p: (l, p, 0)),
        out_shape=jax.ShapeDtypeStruct((depth, n_parts * N_KEYS, d), BF16),
        compiler_params=_cparams("parallel", "parallel"),
        name="peer_fold_keys",
    )(sub_keys, w_pq)


def _head_rms_scale(q, sel, sel_t):
    hi, lo = _split_bf16(q * q)
    ss = jnp.dot(hi, sel, preferred_element_type=F32) + jnp.dot(lo, sel, preferred_element_type=F32)
    r = lax.rsqrt(ss * (1.0 / HEAD_DIM) + EPS)
    r_hi, r_lo = _split_bf16(r)
    return jnp.dot(r_hi, sel_t, preferred_element_type=F32) + jnp.dot(r_lo, sel_t, preferred_element_type=F32)


def _rope(x, cos, sin_signed):
    n = x.shape[1]
    lane = lax.broadcasted_iota(jnp.int32, x.shape, 1)
    upper = (lane & ROT_HALF) != 0
    partner = jnp.where(upper, pltpu.roll(x, ROT_HALF, axis=1), pltpu.roll(x, n - ROT_HALF, axis=1))
    return x * cos + partner * sin_signed


def _qkv_kernel(x_ref, mod_ref, nw_ref, w_ref, qg_ref, kg_ref, selq_ref, selqt_ref, selk_ref, selkt_ref,
                cos_ref, sin_ref, q_ref, k_ref, v_ref, kf_ref, vf_ref):
    dq = N_HEADS * HEAD_DIM
    dk = N_KV_HEADS * HEAD_DIM
    h = _norm_mod(x_ref[...], nw_ref[...], mod_ref[0:1, :], mod_ref[1:2, :]).astype(BF16)
    qkv = jnp.dot(h, w_ref[...], preferred_element_type=F32)
    q = qkv[:, :dq]
    k = qkv[:, dq:dq + dk]
    v = qkv[:, dq + dk:]
    cos = cos_ref[...]
    sin = sin_ref[...]
    qn = q * _head_rms_scale(q, selq_ref[...], selqt_ref[...]) * qg_ref[...]
    q_ref[...] = _rope(qn, cos, sin).astype(BF16)
    kn = k * _head_rms_scale(k, selk_ref[...], selkt_ref[...]) * kg_ref[...]
    kf_ref[...] = kn
    vf_ref[...] = v
    kr = _rope(kn, cos[:, :dk], sin[:, :dk]).astype(BF16)
    vb = v.astype(BF16)
    for g in range(N_KV_HEADS):
        k_ref[g] = kr[:, g * HEAD_DIM:(g + 1) * HEAD_DIM]
        v_ref[g] = vb[:, g * HEAD_DIM:(g + 1) * HEAD_DIM]


def _softmax_pv(qh, parts):
    scores = [_dot_nt(qh, k) for k, _ in parts]
    m = scores[0].max(axis=-1, keepdims=True)
    for s in scores[1:]:
        m = jnp.maximum(m, s.max(axis=-1, keepdims=True))
    num = None
    den = None
    for s, (_, v) in zip(scores, parts):
        p = jnp.exp(s - m)
        d = p.sum(axis=-1, keepdims=True)
        o = jnp.dot(p.astype(BF16), v, preferred_element_type=F32)
        num = o if num is None else num + o
        den = d if den is None else den + d
    return num / den


def _attn_kernel(q_ref, ka_ref, va_ref, kb_ref, vb_ref, kc_ref, vc_ref, o_ref, *, n_prompt_blocks):
    u = pl.program_id(0)
    group = N_HEADS // N_KV_HEADS

    @pl.when(u < n_prompt_blocks)
    def _():
        parts = [(ka_ref[...], va_ref[...])]
        for hh in range(group):
            sl = slice(hh * HEAD_DIM, (hh + 1) * HEAD_DIM)
            o_ref[:, sl] = _softmax_pv(q_ref[:, sl], parts).astype(BF16)

    @pl.when(u >= n_prompt_blocks)
    def _():
        parts = [(kb_ref[...], vb_ref[...]), (kc_ref[...], vc_ref[...])]
        for hh in range(group):
            sl = slice(hh * HEAD_DIM, (hh + 1) * HEAD_DIM)
            o_ref[:, sl] = _softmax_pv(q_ref[:, sl], parts).astype(BF16)


def _proj_res_kernel(a_ref, w_ref, x_ref, mod_ref, o_ref):
    o_ref[...] = x_ref[...] + mod_ref[2:3, :] * jnp.dot(a_ref[...], w_ref[...], preferred_element_type=F32)


def _conv_in_kernel(x_ref, mod_ref, nw_ref, w_ref, u_ref, gb_ref):
    d = x_ref.shape[1]
    h = _norm_mod(x_ref[...], nw_ref[...], mod_ref[0:1, :], mod_ref[1:2, :]).astype(BF16)
    bcx = jnp.dot(h, w_ref[...], preferred_element_type=F32)
    gb_ref[...] = bcx[:, :d].astype(BF16)
    u_ref[...] = bcx[:, d:2 * d] * bcx[:, 2 * d:]


def _conv_out_kernel(u_ref, up_ref, un_ref, gb_ref, cw_ref, cb_ref, w_ref, x_ref, mod_ref, o_ref, *,
                     n_prompt_blocks, blocks_per_prompt_seq, blocks_per_sample_seq):
    tb = u_ref.shape[0]
    b = pl.program_id(0)
    bs = b - n_prompt_blocks
    pos = jnp.where(b < n_prompt_blocks, b % blocks_per_prompt_seq, bs % blocks_per_sample_seq)
    per_seq = jnp.where(b < n_prompt_blocks, blocks_per_prompt_seq, blocks_per_sample_seq)
    u = u_ref[...]
    prev_row = jnp.where(pos == 0, 0.0, up_ref[7:8, :])
    next_row = jnp.where(pos == per_seq - 1, 0.0, un_ref[0:1, :])
    row = lax.broadcasted_iota(jnp.int32, u.shape, 0)
    u_m1 = jnp.where(row == 0, prev_row, pltpu.roll(u, 1, axis=0))
    u_p1 = jnp.where(row == tb - 1, next_row, pltpu.roll(u, tb - 1, axis=0))
    conv = u_m1 * cw_ref[0:1, :] + u * cw_ref[1:2, :] + u_p1 * cw_ref[2:3, :] + cb_ref[...]
    y = (gb_ref[...].astype(F32) * conv).astype(BF16)
    o_ref[...] = x_ref[...] + mod_ref[2:3, :] * jnp.dot(y, w_ref[...], preferred_element_type=F32)


def _oddeven_merge(lo, hi, r):
    step = r * 2
    if step < hi - lo:
        yield from _oddeven_merge(lo, hi, step)
        yield from _oddeven_merge(lo + r, hi, step)
        yield from [(i, i + r) for i in range(lo + r, hi - r, step)]
    else:
        yield (lo, lo + r)


def _oddeven_sort(lo, hi):
    if hi - lo >= 1:
        mid = lo + (hi - lo) // 2
        yield from _oddeven_sort(lo, mid)
        yield from _oddeven_sort(mid + 1, hi)
        yield from _oddeven_merge(lo, hi, 1)


def _bitonic_merge_pairs(n):
    out, s = [], n // 2
    while s >= 1:
        out += [(i, i + s) for i in range(n) if (i // s) % 2 == 0]
        s //= 2
    return out


SORT16 = tuple(_oddeven_sort(0, 15))
SORT8 = tuple(_oddeven_sort(0, 7))
BITONIC16 = tuple(_bitonic_merge_pairs(16))


def _apply_net(net, v):
    v = list(v)
    for i, j in net:
        hi, lo = jnp.maximum(v[i], v[j]), jnp.minimum(v[i], v[j])
        v[i], v[j] = hi, lo
    return v


def _merge_top16(a, b):
    return _apply_net(BITONIC16, [jnp.maximum(a[i], b[PEER_TOPK - 1 - i]) for i in range(PEER_TOPK)])


def _sorted_top16(x):
    v = _apply_net(SORT16, [x[SUBLANES * r:SUBLANES * (r + 1)] for r in range(N_KEYS // SUBLANES)])
    for shift in (4, 2, 1):
        v = _merge_top16(v, [pltpu.roll(u, shift, axis=0) for u in v])
    return v


def _route_tables(s1, s2):
    t = s1.shape[1]
    a = _sorted_top16(s1)
    b = _sorted_top16(s2)
    neg = jnp.full((SUBLANES, t), -jnp.inf, F32)
    c = {(p, q): a[p - 1] + b[q - 1] for p in range(1, PEER_TOPK + 1) for q in range(1, PEER_TOPK // p + 1)}
    row1 = [c[(1, q)] for q in range(1, 17)]
    col1 = [c[(p, 1)] for p in range(2, 17)] + [neg]
    row2 = [c[(2, q)] for q in range(2, 9)] + [neg]
    col2 = [c[(p, 2)] for p in range(3, 9)] + [neg, neg]
    rest = [c[(3, 3)], c[(3, 4)], c[(3, 5)], c[(4, 3)], c[(5, 3)], c[(4, 4)], neg, neg]
    m2 = _apply_net(BITONIC16, row2 + col2[::-1])
    m3 = _apply_net(SORT8, rest) + [neg] * 8
    top = _merge_top16(_merge_top16(row1, col1), _merge_top16(m2, m3))
    tau = top[PEER_TOPK - 1]
    z = jnp.zeros((SUBLANES, t), F32)
    for v in top:
        z = z + jnp.exp(v - top[0])
    inv_z = 1.0 / z
    n_of_rank = []
    for p in range(1, PEER_TOPK + 1):
        cnt = jnp.zeros((SUBLANES, t), F32)
        for q in range(1, PEER_TOPK // p + 1):
            cnt = cnt + jnp.where(c[(p, q)] >= tau, 1.0, 0.0)
        n_of_rank.append(cnt)
    n_rows, e1_rows, r2_rows, e2_rows = [], [], [], []
    for r in range(N_KEYS // SUBLANES):
        x1 = s1[SUBLANES * r:SUBLANES * (r + 1)]
        x2 = s2[SUBLANES * r:SUBLANES * (r + 1)]
        n = jnp.zeros((SUBLANES, t), F32)
        r2 = jnp.full((SUBLANES, t), float(PEER_TOPK), F32)
        for q in range(PEER_TOPK - 1, -1, -1):
            n = jnp.where(x1 >= a[q], n_of_rank[q], n)
            r2 = jnp.where(x2 >= b[q], float(q), r2)
        n_rows.append(n)
        r2_rows.append(r2)
        e1_rows.append(jnp.exp(x1 - a[0]) * inv_z)
        e2_rows.append(jnp.exp(x2 - b[0]))
    cat = lambda rows: jnp.concatenate(rows, axis=0)
    return cat(n_rows), cat(e1_rows), cat(r2_rows), cat(e2_rows)


def _route_kernel(x_ref, mod_ref, nw_ref, wf_ref, h_ref, n_ref, e1_ref, r2_ref, e2_ref, s_ref):
    h = _norm_mod(x_ref[...], nw_ref[...], mod_ref[3:4, :], mod_ref[4:5, :]).astype(BF16)
    h_ref[...] = h
    s_ref[...] = _dot_nt(wf_ref[...], h)
    t = s_ref.shape[1]

    def per_head(hd, carry):
        r1 = pl.multiple_of(hd * 2 * N_KEYS, 2 * N_KEYS)
        for l in range(t // ROUTE_LANES):
            ls = slice(l * ROUTE_LANES, (l + 1) * ROUTE_LANES)
            s1 = s_ref[pl.ds(r1, N_KEYS), ls]
            s2 = s_ref[pl.ds(r1 + N_KEYS, N_KEYS), ls]
            n, e1, r2, e2 = _route_tables(s1, s2)
            n_ref[hd, :, ls] = n
            e1_ref[hd, :, ls] = e1
            r2_ref[hd, :, ls] = r2.astype(BF16)
            e2_ref[hd, :, ls] = e2.astype(BF16)
        return carry

    lax.fori_loop(0, PEER_HEADS, per_head, 0)


def _dense_kernel(h_ref, u_ref, v_ref, n_ref, e1_ref, r2_ref, e2_ref, x_ref, mod_ref, o_ref,
                  acc_ref, a_ref, p_ref):
    c = pl.program_id(1)
    n_rows = u_ref.shape[0] // N_KEYS
    tb = h_ref.shape[0]
    sub = SUBLANES

    @pl.when(c == 0)
    def _():
        acc_ref[...] = jnp.zeros_like(acc_ref)

    a_ref[...] = _dot_nt(u_ref[...], h_ref[...])

    for l in range(tb // MASK_LANES):
        ls = slice(l * MASK_LANES, (l + 1) * MASK_LANES)
        for ii in range(n_rows):
            rs = slice(ii * N_KEYS, (ii + 1) * N_KEYS)
            w = jnp.zeros((N_KEYS // sub, sub, MASK_LANES), BF16)
            for hd in range(PEER_HEADS):
                nb = jnp.broadcast_to(n_ref[hd, ii:ii + 1, ls], (sub, MASK_LANES)).astype(BF16)
                eb = jnp.broadcast_to(e1_ref[hd, ii:ii + 1, ls], (sub, MASK_LANES)).astype(BF16)
                r2 = r2_ref[hd, :, ls].reshape(N_KEYS // sub, sub, MASK_LANES)
                e2 = e2_ref[hd, :, ls].reshape(N_KEYS // sub, sub, MASK_LANES)
                w = w + jnp.where(r2 < nb[None], e2 * eb[None], jnp.zeros_like(e2))
            p_ref[rs, ls] = _gelu_tanh(a_ref[rs, ls].astype(BF16)) * w.reshape(N_KEYS, MASK_LANES)
    acc_ref[...] += _dot_tn(p_ref[...], v_ref[...])

    @pl.when(c == pl.num_programs(1) - 1)
    def _():
        o_ref[...] = x_ref[...] + mod_ref[5:6, :] * acc_ref[...]


def _final_norm_kernel(x_ref, nw_ref, o_ref):
    x = x_ref[...]
    ms = jnp.mean(x * x, axis=-1, keepdims=True)
    o_ref[...] = x * lax.rsqrt(ms + EPS) * nw_ref[...]


def _rope_tables(seq_len, n_identity):
    axis_dim = HEAD_DIM // 2
    n_rows = seq_len // GRID_W
    row = jnp.repeat(jnp.arange(n_rows, dtype=F32), GRID_W)
    col = jnp.tile(jnp.arange(GRID_W, dtype=F32), n_rows)
    inv = ROPE_THETA ** (-jnp.arange(0, axis_dim, 2, dtype=F32) / axis_dim)
    ang_r = row[:, None] * inv
    ang_c = col[:, None] * inv
    cos = jnp.concatenate([jnp.cos(ang_r)] * 2 + [jnp.cos(ang_c)] * 2, axis=-1)
    sin = jnp.concatenate([-jnp.sin(ang_r), jnp.sin(ang_r), -jnp.sin(ang_c), jnp.sin(ang_c)], axis=-1)
    cos = jnp.concatenate([jnp.ones((n_identity, HEAD_DIM), F32), cos], axis=0)
    sin = jnp.concatenate([jnp.zeros((n_identity, HEAD_DIM), F32), sin], axis=0)
    return jnp.tile(cos, (1, N_HEADS)), jnp.tile(sin, (1, N_HEADS))


def _head_selectors(n_heads):
    c = jnp.arange(n_heads * HEAD_DIM)[:, None] // HEAD_DIM
    sel = (c == jnp.arange(128)[None, :]).astype(BF16)
    return sel, sel.T


def _peer_route(x, mods, norm_ffn3, wf_all, layer, cond_row):
    tt, d = x.shape
    tb = TOKEN_BLOCK
    n_keys2 = PEER_HEADS * 2 * N_KEYS
    tab = lambda dtype: jax.ShapeDtypeStruct((PEER_HEADS, N_KEYS, tt), dtype)
    tab_spec = pl.BlockSpec((PEER_HEADS, N_KEYS, tb), lambda b: (0, 0, b))
    row_spec = pl.BlockSpec((tb, d), lambda b: (b, 0))
    h, *tabs = pl.pallas_call(
        _route_kernel,
        grid=(tt // tb,),
        in_specs=[row_spec,
                  pl.BlockSpec((None, None, N_MOD, d), lambda b: (layer, cond_row(b, tb), 0, 0)),
                  pl.BlockSpec((None, 1, d), lambda b: (layer, 0, 0)),
                  pl.BlockSpec((None, n_keys2, d), lambda b: (layer, 0, 0))],
        out_specs=[row_spec, tab_spec, tab_spec, tab_spec, tab_spec],
        out_shape=[jax.ShapeDtypeStruct((tt, d), BF16), tab(F32), tab(F32), tab(BF16), tab(BF16)],
        scratch_shapes=[pltpu.VMEM((n_keys2, tb), F32)],
        compiler_params=_cparams("parallel"),
        name="peer_route",
    )(x, mods, norm_ffn3, wf_all)
    return h, tabs


def _peer_dense(h, u_bf, v_bf, tabs, x, mods, layer, cond_row):
    tt, d = x.shape
    n_exp = u_bf.shape[1]
    dt, de = DENSE_TOKENS, DENSE_CHUNK
    n_tab, e1_tab, r2_tab, e2_tab = tabs
    dtab = pl.BlockSpec((PEER_HEADS, N_KEYS, dt), lambda b, e: (0, 0, b))
    rtab = pl.BlockSpec((PEER_HEADS, de // N_KEYS, dt), lambda b, e: (0, e, b))
    return pl.pallas_call(
        _dense_kernel,
        grid=(tt // dt, n_exp // de),
        in_specs=[
            pl.BlockSpec((dt, d), lambda b, e: (b, 0)),
            pl.BlockSpec((None, de, d), lambda b, e: (layer, e, 0)),
            pl.BlockSpec((None, de, d), lambda b, e: (layer, e, 0)),
            rtab, rtab, dtab, dtab,
            pl.BlockSpec((dt, d), lambda b, e: (b, 0)),
            pl.BlockSpec((None, None, N_MOD, d), lambda b, e: (layer, cond_row(b, dt), 0, 0)),
        ],
        out_specs=pl.BlockSpec((dt, d), lambda b, e: (b, 0)),
        out_shape=jax.ShapeDtypeStruct((tt, d), F32),
        scratch_shapes=[pltpu.VMEM((dt, d), F32), pltpu.VMEM((de, dt), F32), pltpu.VMEM((de, dt), BF16)],
        compiler_params=_cparams("parallel", "arbitrary"),
        name="peer_dense",
    )(h, u_bf, v_bf, n_tab, e1_tab, r2_tab, e2_tab, x, mods)


def kernel(x_prompt, x_sample, cache_k, cache_v, c, c_ctx, w_ada, b_ada, norm_mix, norm_ffn, norm_final,
           w_qkv, q_gain, k_gain, w_o, w_bcx, conv_w, conv_b, w_conv_out, w_pq, sub_keys, u_exp, v_exp):
    batch, seq, d = x_prompt.shape
    dec_batch, dec_seq, _ = x_sample.shape
    depth = w_ada.shape[0]
    past = cache_k.shape[2]
    tp = batch * seq
    ts = dec_batch * dec_seq
    tt = tp + ts
    tb = TOKEN_BLOCK
    assert seq % tb == 0 and dec_seq % tb == 0 and dec_batch + 1 <= N_COND_ROWS
    assert tp % DENSE_TOKENS == 0 and dec_seq % DENSE_TOKENS == 0
    assert d == N_HEADS * HEAD_DIM and seq == tb and past == tb and tp % dec_seq == 0
    npb = tp // tb
    bpp = seq // tb
    bps = dec_seq // tb
    nblk = tt // tb
    dq = N_HEADS * HEAD_DIM
    dk = N_KV_HEADS * HEAD_DIM
    n_exp = u_exp.shape[1]
    assert n_exp == N_KEYS * N_KEYS and n_exp % DENSE_CHUNK == 0

    def cond_row(b, rows=tb):
        t0 = b * rows
        return jnp.where(t0 < tp, 0, 1 + (t0 - tp) // dec_seq)

    def mod_spec(layer):
        return pl.BlockSpec((None, None, N_MOD, d), lambda b: (layer, cond_row(b), 0, 0))

    def row_spec(width, dtype_rows=tb):
        return pl.BlockSpec((dtype_rows, width), lambda b: (b, 0))

    def layer_vec(layer, width):
        return pl.BlockSpec((None, 1, width), lambda b: (layer, 0, 0))

    def const2(shape):
        return pl.BlockSpec(shape, lambda b: (0, 0))

    x = jnp.concatenate([x_prompt.reshape(tp, d), x_sample.reshape(ts, d)], axis=0)
    cond = jnp.zeros((N_COND_ROWS, d), F32).at[0].set(c_ctx).at[1:1 + dec_batch].set(c)
    mods = _modulation(cond, w_ada, b_ada).reshape(depth, N_COND_ROWS, N_MOD, d)

    wf_all = _fold_keys(w_pq, sub_keys)
    u_bf = u_exp.astype(BF16)
    v_bf = v_exp.astype(BF16)
    w_qkv_bf = w_qkv.astype(BF16)
    w_o_bf = w_o.astype(BF16)
    w_bcx_bf = w_bcx.astype(BF16)
    w_co_bf = w_conv_out.astype(BF16)
    norm_mix3 = norm_mix.reshape(depth, 1, d)
    norm_ffn3 = norm_ffn.reshape(depth, 1, d)

    cos_t, sin_t = _rope_tables(dec_seq, tb)
    selq, selq_t = _head_selectors(N_HEADS)
    selk, selk_t = _head_selectors(N_KV_HEADS)
    scale = HEAD_DIM ** -0.5
    qg_all = jnp.tile(q_gain, (1, N_HEADS)).reshape(-1, 1, dq) * scale
    kg_all = jnp.tile(k_gain, (1, N_KV_HEADS)).reshape(-1, 1, dk)
    ck_all = jnp.transpose(cache_k, (1, 0, 3, 2, 4)).astype(BF16)
    cv_all = jnp.transpose(cache_v, (1, 0, 3, 2, 4)).astype(BF16)

    def rope_block(b):
        return jnp.where(b < npb, 0, 1 + (b - npb) % bps)

    new_k, new_v = [], []
    for i in range(depth):
        j = i // 2
        if i % 2 == 0:
            q, k_hm, v_hm, kf, vf = pl.pallas_call(
                _qkv_kernel,
                grid=(nblk,),
                in_specs=[
                    row_spec(d), mod_spec(i), layer_vec(i, d),
                    pl.BlockSpec((None, d, dq + 2 * dk), lambda b, j=j: (j, 0, 0)),
                    pl.BlockSpec((None, 1, dq), lambda b, j=j: (j, 0, 0)),
                    pl.BlockSpec((None, 1, dk), lambda b, j=j: (j, 0, 0)),
                    const2((dq, 128)), const2((128, dq)), const2((dk, 128)), const2((128, dk)),
                    pl.BlockSpec((tb, dq), lambda b: (rope_block(b), 0)),
                    pl.BlockSpec((tb, dq), lambda b: (rope_block(b), 0)),
                ],
                out_specs=[
                    row_spec(dq),
                    pl.BlockSpec((N_KV_HEADS, tb, HEAD_DIM), lambda b: (0, b, 0)),
                    pl.BlockSpec((N_KV_HEADS, tb, HEAD_DIM), lambda b: (0, b, 0)),
                    row_spec(dk), row_spec(dk),
                ],
                out_shape=[
                    jax.ShapeDtypeStruct((tt, dq), BF16),
                    jax.ShapeDtypeStruct((N_KV_HEADS, tt, HEAD_DIM), BF16),
                    jax.ShapeDtypeStruct((N_KV_HEADS, tt, HEAD_DIM), BF16),
                    jax.ShapeDtypeStruct((tt, dk), F32),
                    jax.ShapeDtypeStruct((tt, dk), F32),
                ],
                compiler_params=_cparams("parallel"),
                name="qkv_project",
            )(x, mods, norm_mix3, w_qkv_bf, qg_all, kg_all, selq, selq_t, selk, selk_t, cos_t, sin_t)
            new_k.append(kf[:tp].reshape(batch, seq, N_KV_HEADS, HEAD_DIM))
            new_v.append(vf[:tp].reshape(batch, seq, N_KV_HEADS, HEAD_DIM))

            def sample_seq(u):
                return jnp.maximum(u - npb, 0) // bps

            kv_a = pl.BlockSpec((None, tb, HEAD_DIM), lambda u, g: (g, jnp.minimum(u, npb - 1), 0))
            kv_b = pl.BlockSpec((None, dec_seq, HEAD_DIM), lambda u, g: (g, tp // dec_seq + sample_seq(u), 0))
            kv_c = pl.BlockSpec((None, None, None, past, HEAD_DIM), lambda u, g, j=j: (j, sample_seq(u), g, 0, 0))
            attn = pl.pallas_call(
                functools.partial(_attn_kernel, n_prompt_blocks=npb),
                grid=(nblk, N_KV_HEADS),
                in_specs=[pl.BlockSpec((tb, dk), lambda u, g: (u, g)), kv_a, kv_a, kv_b, kv_b, kv_c, kv_c],
                out_specs=pl.BlockSpec((tb, dk), lambda u, g: (u, g)),
                out_shape=jax.ShapeDtypeStruct((tt, dq), BF16),
                compiler_params=_cparams("parallel", "parallel"),
                name="attention",
            )(q, k_hm, v_hm, k_hm, v_hm, ck_all, cv_all)

            x = pl.pallas_call(
                _proj_res_kernel,
                grid=(nblk,),
                in_specs=[row_spec(dq), pl.BlockSpec((None, dq, d), lambda b, j=j: (j, 0, 0)), row_spec(d), mod_spec(i)],
                out_specs=row_spec(d),
                out_shape=jax.ShapeDtypeStruct((tt, d), F32),
                compiler_params=_cparams("parallel"),
                name="attn_out_project",
            )(attn, w_o_bf, x, mods)
        else:
            u, gb = pl.pallas_call(
                _conv_in_kernel,
                grid=(nblk,),
                in_specs=[row_spec(d), mod_spec(i), layer_vec(i, d),
                          pl.BlockSpec((None, d, 3 * d), lambda b, j=j: (j, 0, 0))],
                out_specs=[row_spec(d), row_spec(d)],
                out_shape=[jax.ShapeDtypeStruct((tt, d), F32), jax.ShapeDtypeStruct((tt, d), BF16)],
                compiler_params=_cparams("parallel"),
                name="conv_in_project",
            )(x, mods, norm_mix3, w_bcx_bf)
            halo = tb // 8
            x = pl.pallas_call(
                functools.partial(_conv_out_kernel, n_prompt_blocks=npb, blocks_per_prompt_seq=bpp,
                                  blocks_per_sample_seq=bps),
                grid=(nblk,),
                in_specs=[
                    row_spec(d),
                    pl.BlockSpec((8, d), lambda b: (jnp.maximum(b * halo - 1, 0), 0)),
                    pl.BlockSpec((8, d), lambda b: (jnp.minimum((b + 1) * halo, nblk * halo - 1), 0)),
                    row_spec(d),
                    pl.BlockSpec((None, 3, d), lambda b, j=j: (j, 0, 0)),
                    pl.BlockSpec((None, 1, d), lambda b, j=j: (j, 0, 0)),
                    pl.BlockSpec((None, d, d), lambda b, j=j: (j, 0, 0)),
                    row_spec(d), mod_spec(i),
                ],
                out_specs=row_spec(d),
                out_shape=jax.ShapeDtypeStruct((tt, d), F32),
                compiler_params=_cparams("parallel"),
                name="conv_out_project",
            )(u, u, u, gb, conv_w, conv_b.reshape(-1, 1, d), w_co_bf, x, mods)

        h, tabs = _peer_route(x, mods, norm_ffn3, wf_all, i, cond_row)
        x = _peer_dense(h, u_bf, v_bf, tabs, x, mods, i, cond_row)

    y = pl.pallas_call(
        _final_norm_kernel,
        grid=(nblk,),
        in_specs=[row_spec(d), const2((1, d))],
        out_specs=row_spec(d),
        out_shape=jax.ShapeDtypeStruct((tt, d), F32),
        compiler_params=_cparams("parallel"),
        name="final_norm",
    )(x, norm_final.reshape(1, d))

    y_prompt = y[:tp].reshape(batch, seq, d)
    y_sample = y[tp:].reshape(dec_batch, dec_seq, d)
    return (y_prompt, y_sample, jnp.stack(new_k, axis=1), jnp.stack(new_v, axis=1))
```

```python
import functools
import math

import jax
import jax.numpy as jnp
from jax import lax
from jax.experimental import pallas as pl
from jax.experimental.pallas import tpu as pltpu

F32 = jnp.float32
BF16 = jnp.bfloat16

GRID_W = 64
N_HEADS = 16
N_KV_HEADS = 4
HEAD_DIM = 64
ROT_HALF = 16
ROPE_THETA = 10000.0
PEER_HEADS = 8
N_KEYS = 128
PEER_TOPK = 16
N_MOD = 6
EPS = 1e-6
N_COND_ROWS = 8
SUBLANES = 8

TOKEN_BLOCK = 256
DENSE_TOKENS = 512
DENSE_CHUNK = 2048
MASK_LANES = 256
ROUTE_LANES = 128
VMEM_LIMIT = 52 * 1024 * 1024


def _cparams(*sem):
    return pltpu.CompilerParams(dimension_semantics=sem, vmem_limit_bytes=VMEM_LIMIT)


def _split_bf16(a):
    hi = a.astype(BF16)
    lo = (a - hi.astype(F32)).astype(BF16)
    return hi, lo


def _norm_mod(x, nw, shift, scale):
    ms = jnp.mean(x * x, axis=-1, keepdims=True)
    return (x * lax.rsqrt(ms + EPS)) * nw * (1.0 + scale) + shift


def _gelu_tanh(x):
    c0 = math.sqrt(2.0 / math.pi)
    inner = x * (c0 + (c0 * 0.044715) * (x * x))
    return (0.5 * x) * (1.0 + jnp.tanh(inner))


def _dot_nt(a, b):
    return lax.dot_general(a, b, (((1,), (1,)), ((), ())), preferred_element_type=F32)


def _dot_tn(a, b):
    return lax.dot_general(a, b, (((0,), (0,)), ((), ())), preferred_element_type=F32)


def _mod_kernel(c_ref, w_ref, b_ref, o_ref):
    c = c_ref[...]
    s_hi, s_lo = _split_bf16(jax.nn.silu(c))
    w_hi, w_lo = _split_bf16(w_ref[...])
    acc = jnp.dot(s_hi, w_hi, preferred_element_type=F32)
    acc += jnp.dot(s_hi, w_lo, preferred_element_type=F32)
    acc += jnp.dot(s_lo, w_hi, preferred_element_type=F32)
    o_ref[...] = acc + b_ref[...]


def _modulation(cond, w_ada, b_ada):
    depth, d, nd = w_ada.shape
    nb = 1536
    return pl.pallas_call(
        _mod_kernel,
        grid=(depth, nd // nb),
        in_specs=[
            pl.BlockSpec((N_COND_ROWS, d), lambda l, n: (0, 0)),
            pl.BlockSpec((None, d, nb), lambda l, n: (l, 0, n)),
            pl.BlockSpec((None, 1, nb), lambda l, n: (l, 0, n)),
        ],
        out_specs=pl.BlockSpec((None, N_COND_ROWS, nb), lambda l, n: (l, 0, n)),
        out_shape=jax.ShapeDtypeStruct((depth, N_COND_ROWS, nd), F32),
        compiler_params=_cparams("parallel", "parallel"),
        name="adaln_modulation",
    )(cond, w_ada, b_ada.reshape(depth, 1, nd))


def _fold_kernel(sk_ref, w_ref, o_ref):
    k_hi, k_lo = _split_bf16(sk_ref[...])
    w_hi, w_lo = _split_bf16(w_ref[...])
    acc = _dot_nt(k_hi, w_hi) + _dot_nt(k_hi, w_lo) + _dot_nt(k_lo, w_hi)
    o_ref[...] = acc.astype(BF16)


def _fold_keys(w_pq, sub_keys):
    depth, d, _ = w_pq.shape
    dkh = sub_keys.shape[-1]
    n_parts = PEER_HEADS * 2
    return pl.pallas_call(
        _fold_kernel,
        grid=(depth, n_parts),
        in_specs=[
            pl.BlockSpec((None, None, None, N_KEYS, dkh), lambda l, p: (l, p // 2, p % 2, 0, 0)),
            pl.BlockSpec((None, d, dkh), lambda l, p: (l, 0, p)),
        ],
        out_specs=pl.BlockSpec((None, N_KEYS, d), lambda l, p: (l, p, 0)),
        out_shape=jax.ShapeDtypeStruct((depth, n_parts * N_KEYS, d), BF16),
        compiler_params=_cparams("parallel", "parallel"),
        name="peer_fold_keys",
    )(sub_keys, w_pq)


def _head_rms_scale(q, sel, sel_t):
    hi, lo = _split_bf16(q * q)
    ss = jnp.dot(hi, sel, preferred_element_type=F32) + jnp.dot(lo, sel, preferred_element_type=F32)
    r = lax.rsqrt(ss * (1.0 / HEAD_DIM) + EPS)
    r_hi, r_lo = _split_bf16(r)
    return jnp.dot(r_hi, sel_t, preferred_element_type=F32) + jnp.dot(r_lo, sel_t, preferred_element_type=F32)


def _rope(x, cos, sin_signed):
    n = x.shape[1]
    lane = lax.broadcasted_iota(jnp.int32, x.shape, 1)
    upper = (lane & ROT_HALF) != 0
    partner = jnp.where(upper, pltpu.roll(x, ROT_HALF, axis=1), pltpu.roll(x, n - ROT_HALF, axis=1))
    return x * cos + partner * sin_signed


def _qkv_kernel(x_ref, mod_ref, nw_ref, w_ref, qg_ref, kg_ref, selq_ref, selqt_ref, selk_ref, selkt_ref,
                cos_ref, sin_ref, q_ref, k_ref, v_ref, kf_ref, vf_ref):
    dq = N_HEADS * HEAD_DIM
    dk = N_KV_HEADS * HEAD_DIM
    h = _norm_mod(x_ref[...], nw_ref[...], mod_ref[0:1, :], mod_ref[1:2, :]).astype(BF16)
    qkv = jnp.dot(h, w_ref[...], preferred_element_type=F32)
    q = qkv[:, :dq]
    k = qkv[:, dq:dq + dk]
    v = qkv[:, dq + dk:]
    cos = cos_ref[...]
    sin = sin_ref[...]
    qn = q * _head_rms_scale(q, selq_ref[...], selqt_ref[...]) * qg_ref[...]
    q_ref[...] = _rope(qn, cos, sin).astype(BF16)
    kn = k * _head_rms_scale(k, selk_ref[...], selkt_ref[...]) * kg_ref[...]
    kf_ref[...] = kn
    vf_ref[...] = v
    kr = _rope(kn, cos[:, :dk], sin[:, :dk]).astype(BF16)
    vb = v.astype(BF16)
    for g in range(N_KV_HEADS):
        k_ref[g] = kr[:, g * HEAD_DIM:(g + 1) * HEAD_DIM]
        v_ref[g] = vb[:, g * HEAD_DIM:(g + 1) * HEAD_DIM]


def _softmax_pv(qh, parts):
    scores = [_dot_nt(qh, k) for k, _ in parts]
    m = scores[0].max(axis=-1, keepdims=True)
    for s in scores[1:]:
        m = jnp.maximum(m, s.max(axis=-1, keepdims=True))
    num = None
    den = None
    for s, (_, v) in zip(scores, parts):
        p = jnp.exp2(s - m)
        d = p.sum(axis=-1, keepdims=True)
        o = jnp.dot(p.astype(BF16), v, preferred_element_type=F32)
        num = o if num is None else num + o
        den = d if den is None else den + d
    return num / den


def _attn_kernel(q_ref, ka_ref, va_ref, kb_ref, vb_ref, kc_ref, vc_ref, o_ref, *, n_prompt_blocks):
    u = pl.program_id(0)
    group = N_HEADS // N_KV_HEADS

    @pl.when(u < n_prompt_blocks)
    def _():
        parts = [(ka_ref[...], va_ref[...])]
        for hh in range(group):
            sl = slice(hh * HEAD_DIM, (hh + 1) * HEAD_DIM)
            o_ref[:, sl] = _softmax_pv(q_ref[:, sl], parts).astype(BF16)

    @pl.when(u >= n_prompt_blocks)
    def _():
        parts = [(kb_ref[...], vb_ref[...]), (kc_ref[...], vc_ref[...])]
        for hh in range(group):
            sl = slice(hh * HEAD_DIM, (hh + 1) * HEAD_DIM)
            o_ref[:, sl] = _softmax_pv(q_ref[:, sl], parts).astype(BF16)


def _proj_res_kernel(a_ref, w_ref, x_ref, mod_ref, o_ref):
    o_ref[...] = x_ref[...] + mod_ref[2:3, :] * jnp.dot(a_ref[...], w_ref[...], preferred_element_type=F32)


def _conv_in_kernel(x_ref, mod_ref, nw_ref, w_ref, u_ref, gb_ref):
    d = x_ref.shape[1]
    h = _norm_mod(x_ref[...], nw_ref[...], mod_ref[0:1, :], mod_ref[1:2, :]).astype(BF16)
    bcx = jnp.dot(h, w_ref[...], preferred_element_type=F32)
    gb_ref[...] = bcx[:, :d].astype(BF16)
    u_ref[...] = bcx[:, d:2 * d] * bcx[:, 2 * d:]


def _conv_out_kernel(u_ref, up_ref, un_ref, gb_ref, cw_ref, cb_ref, w_ref, x_ref, mod_ref, o_ref, *,
                     n_prompt_blocks, blocks_per_prompt_seq, blocks_per_sample_seq):
    tb = u_ref.shape[0]
    b = pl.program_id(0)
    bs = b - n_prompt_blocks
    pos = jnp.where(b < n_prompt_blocks, b % blocks_per_prompt_seq, bs % blocks_per_sample_seq)
    per_seq = jnp.where(b < n_prompt_blocks, blocks_per_prompt_seq, blocks_per_sample_seq)
    u = u_ref[...]
    prev_row = jnp.where(pos == 0, 0.0, up_ref[7:8, :])
    next_row = jnp.where(pos == per_seq - 1, 0.0, un_ref[0:1, :])
    row = lax.broadcasted_iota(jnp.int32, u.shape, 0)
    u_m1 = jnp.where(row == 0, prev_row, pltpu.roll(u, 1, axis=0))
    u_p1 = jnp.where(row == tb - 1, next_row, pltpu.roll(u, tb - 1, axis=0))
    conv = u_m1 * cw_ref[0:1, :] + u * cw_ref[1:2, :] + u_p1 * cw_ref[2:3, :] + cb_ref[...]
    y = (gb_ref[...].astype(F32) * conv).astype(BF16)
    o_ref[...] = x_ref[...] + mod_ref[2:3, :] * jnp.dot(y, w_ref[...], preferred_element_type=F32)


def _oddeven_merge(lo, hi, r):
    step = r * 2
    if step < hi - lo:
        yield from _oddeven_merge(lo, hi, step)
        yield from _oddeven_merge(lo + r, hi, step)
        yield from [(i, i + r) for i in range(lo + r, hi - r, step)]
    else:
        yield (lo, lo + r)


def _oddeven_sort(lo, hi):
    if hi - lo >= 1:
        mid = lo + (hi - lo) // 2
        yield from _oddeven_sort(lo, mid)
        yield from _oddeven_sort(mid + 1, hi)
        yield from _oddeven_merge(lo, hi, 1)


def _bitonic_merge_pairs(n):
    out, s = [], n // 2
    while s >= 1:
        out += [(i, i + s) for i in range(n) if (i // s) % 2 == 0]
        s //= 2
    return out


SORT16 = tuple(_oddeven_sort(0, 15))
SORT8 = tuple(_oddeven_sort(0, 7))
BITONIC16 = tuple(_bitonic_merge_pairs(16))


def _apply_net(net, v):
    v = list(v)
    for i, j in net:
        hi, lo = jnp.maximum(v[i], v[j]), jnp.minimum(v[i], v[j])
        v[i], v[j] = hi, lo
    return v


def _merge_top16(a, b):
    return _apply_net(BITONIC16, [jnp.maximum(a[i], b[PEER_TOPK - 1 - i]) for i in range(PEER_TOPK)])


def _sorted_top16(x):
    v = _apply_net(SORT16, [x[SUBLANES * r:SUBLANES * (r + 1)] for r in range(N_KEYS // SUBLANES)])
    for shift in (4, 2, 1):
        v = _merge_top16(v, [pltpu.roll(u, shift, axis=0) for u in v])
    return v


def _route_tables(s1, s2):
    t = s1.shape[1]
    a = _sorted_top16(s1)
    b = _sorted_top16(s2)
    neg = jnp.full((SUBLANES, t), -jnp.inf, F32)
    c = {(p, q): a[p - 1] + b[q - 1] for p in range(1, PEER_TOPK + 1) for q in range(1, PEER_TOPK // p + 1)}
    row1 = [c[(1, q)] for q in range(1, 17)]
    col1 = [c[(p, 1)] for p in range(2, 17)] + [neg]
    row2 = [c[(2, q)] for q in range(2, 9)] + [neg]
    col2 = [c[(p, 2)] for p in range(3, 9)] + [neg, neg]
    rest = [c[(3, 3)], c[(3, 4)], c[(3, 5)], c[(4, 3)], c[(5, 3)], c[(4, 4)], neg, neg]
    m2 = _apply_net(BITONIC16, row2 + col2[::-1])
    m3 = _apply_net(SORT8, rest) + [neg] * 8
    top = _merge_top16(_merge_top16(row1, col1), _merge_top16(m2, m3))
    tau = top[PEER_TOPK - 1]
    z = jnp.zeros((SUBLANES, t), F32)
    for v in top:
        z = z + jnp.exp(v - top[0])
    inv_z = 1.0 / z
    n_of_rank = []
    for p in range(1, PEER_TOPK + 1):
        cnt = jnp.zeros((SUBLANES, t), F32)
        for q in range(1, PEER_TOPK // p + 1):
            cnt = cnt + jnp.where(c[(p, q)] >= tau, 1.0, 0.0)
        n_of_rank.append(cnt)
    n_rows, e1_rows, r2_rows, e2_rows = [], [], [], []
    for r in range(N_KEYS // SUBLANES):
        x1 = s1[SUBLANES * r:SUBLANES * (r + 1)]
        x2 = s2[SUBLANES * r:SUBLANES * (r + 1)]
        n = jnp.zeros((SUBLANES, t), F32)
        r2 = jnp.full((SUBLANES, t), float(PEER_TOPK), F32)
        for q in range(PEER_TOPK - 1, -1, -1):
            n = jnp.where(x1 >= a[q], n_of_rank[q], n)
            r2 = jnp.where(x2 >= b[q], float(q), r2)
        n_rows.append(n)
        r2_rows.append(r2)
        e1_rows.append(jnp.exp(x1 - a[0]) * inv_z)
        e2_rows.append(jnp.exp(x2 - b[0]))
    cat = lambda rows: jnp.concatenate(rows, axis=0)
    return cat(n_rows), cat(e1_rows), cat(r2_rows), cat(e2_rows)


def _route_kernel(x_ref, mod_ref, nw_ref, wf_ref, h_ref, n_ref, e1_ref, r2_ref, e2_ref, s_ref):
    h = _norm_mod(x_ref[...], nw_ref[...], mod_ref[3:4, :], mod_ref[4:5, :]).astype(BF16)
    h_ref[...] = h
    s_ref[...] = _dot_nt(wf_ref[...], h)
    t = s_ref.shape[1]

    def per_head(hd, carry):
        r1 = pl.multiple_of(hd * 2 * N_KEYS, 2 * N_KEYS)
        for l in range(t // ROUTE_LANES):
            ls = slice(l * ROUTE_LANES, (l + 1) * ROUTE_LANES)
            s1 = s_ref[pl.ds(r1, N_KEYS), ls]
            s2 = s_ref[pl.ds(r1 + N_KEYS, N_KEYS), ls]
            n, e1, r2, e2 = _route_tables(s1, s2)
            n_ref[hd, :, ls] = n
            e1_ref[hd, :, ls] = e1
            r2_ref[hd, :, ls] = r2.astype(BF16)
            e2_ref[hd, :, ls] = e2.astype(BF16)
        return carry

    lax.fori_loop(0, PEER_HEADS, per_head, 0)


def _dense_kernel(h_ref, u_ref, v_ref, n_ref, e1_ref, r2_ref, e2_ref, x_ref, mod_ref, nf_ref, o_ref,
                  acc_ref, a_ref, p_ref, *, final_norm):
    c = pl.program_id(1)
    n_rows = u_ref.shape[0] // N_KEYS
    tb = h_ref.shape[0]
    sub = SUBLANES

    @pl.when(c == 0)
    def _():
        acc_ref[...] = jnp.zeros_like(acc_ref)

    a_ref[...] = _dot_nt(u_ref[...], h_ref[...])

    for l in range(tb // MASK_LANES):
        ls = slice(l * MASK_LANES, (l + 1) * MASK_LANES)
        for ii in range(n_rows):
            rs = slice(ii * N_KEYS, (ii + 1) * N_KEYS)
            w = jnp.zeros((N_KEYS // sub, sub, MASK_LANES), BF16)
            for hd in range(PEER_HEADS):
                nb = jnp.broadcast_to(n_ref[hd, ii:ii + 1, ls], (sub, MASK_LANES)).astype(BF16)
                eb = jnp.broadcast_to(e1_ref[hd, ii:ii + 1, ls], (sub, MASK_LANES)).astype(BF16)
                r2 = r2_ref[hd, :, ls].reshape(N_KEYS // sub, sub, MASK_LANES)
                e2 = e2_ref[hd, :, ls].reshape(N_KEYS // sub, sub, MASK_LANES)
                w = w + jnp.where(r2 < nb[None], e2 * eb[None], jnp.zeros_like(e2))
            p_ref[rs, ls] = _gelu_tanh(a_ref[rs, ls].astype(BF16)) * w.reshape(N_KEYS, MASK_LANES)
    acc_ref[...] += _dot_tn(p_ref[...], v_ref[...])

    @pl.when(c == pl.num_programs(1) - 1)
    def _():
        y = x_ref[...] + mod_ref[5:6, :] * acc_ref[...]
        if final_norm:
            ms = jnp.mean(y * y, axis=-1, keepdims=True)
            y = y * lax.rsqrt(ms + EPS) * nf_ref[...]
        o_ref[...] = y


def _rope_tables(seq_len, n_identity):
    axis_dim = HEAD_DIM // 2
    n_rows = seq_len // GRID_W
    row = jnp.repeat(jnp.arange(n_rows, dtype=F32), GRID_W)
    col = jnp.tile(jnp.arange(GRID_W, dtype=F32), n_rows)
    inv = ROPE_THETA ** (-jnp.arange(0, axis_dim, 2, dtype=F32) / axis_dim)
    ang_r = row[:, None] * inv
    ang_c = col[:, None] * inv
    cos = jnp.concatenate([jnp.cos(ang_r)] * 2 + [jnp.cos(ang_c)] * 2, axis=-1)
    sin = jnp.concatenate([-jnp.sin(ang_r), jnp.sin(ang_r), -jnp.sin(ang_c), jnp.sin(ang_c)], axis=-1)
    cos = jnp.concatenate([jnp.ones((n_identity, HEAD_DIM), F32), cos], axis=0)
    sin = jnp.concatenate([jnp.zeros((n_identity, HEAD_DIM), F32), sin], axis=0)
    return jnp.tile(cos, (1, N_HEADS)), jnp.tile(sin, (1, N_HEADS))


def _head_selectors(n_heads):
    c = jnp.arange(n_heads * HEAD_DIM)[:, None] // HEAD_DIM
    sel = (c == jnp.arange(128)[None, :]).astype(BF16)
    return sel, sel.T


def _peer_route(x, mods, norm_ffn3, wf_all, layer, cond_row):
    tt, d = x.shape
    tb = TOKEN_BLOCK
    n_keys2 = PEER_HEADS * 2 * N_KEYS
    tab = lambda dtype: jax.ShapeDtypeStruct((PEER_HEADS, N_KEYS, tt), dtype)
    tab_spec = pl.BlockSpec((PEER_HEADS, N_KEYS, tb), lambda b: (0, 0, b))
    row_spec = pl.BlockSpec((tb, d), lambda b: (b, 0))
    h, *tabs = pl.pallas_call(
        _route_kernel,
        grid=(tt // tb,),
        in_specs=[row_spec,
                  pl.BlockSpec((None, None, N_MOD, d), lambda b: (layer, cond_row(b, tb), 0, 0)),
                  pl.BlockSpec((None, 1, d), lambda b: (layer, 0, 0)),
                  pl.BlockSpec((None, n_keys2, d), lambda b: (layer, 0, 0))],
        out_specs=[row_spec, tab_spec, tab_spec, tab_spec, tab_spec],
        out_shape=[jax.ShapeDtypeStruct((tt, d), BF16), tab(F32), tab(F32), tab(BF16), tab(BF16)],
        scratch_shapes=[pltpu.VMEM((n_keys2, tb), F32)],
        compiler_params=_cparams("parallel"),
        name="peer_route",
    )(x, mods, norm_ffn3, wf_all)
    return h, tabs


def _peer_dense(h, u_bf, v_bf, tabs, x, mods, layer, cond_row, norm_final, final_norm):
    tt, d = x.shape
    n_exp = u_bf.shape[1]
    dt, de = DENSE_TOKENS, DENSE_CHUNK
    n_tab, e1_tab, r2_tab, e2_tab = tabs
    dtab = pl.BlockSpec((PEER_HEADS, N_KEYS, dt), lambda b, e: (0, 0, b))
    rtab = pl.BlockSpec((PEER_HEADS, de // N_KEYS, dt), lambda b, e: (0, e, b))
    return pl.pallas_call(
        functools.partial(_dense_kernel, final_norm=final_norm),
        grid=(tt // dt, n_exp // de),
        in_specs=[
            pl.BlockSpec((dt, d), lambda b, e: (b, 0)),
            pl.BlockSpec((None, de, d), lambda b, e: (layer, e, 0)),
            pl.BlockSpec((None, de, d), lambda b, e: (layer, e, 0)),
            rtab, rtab, dtab, dtab,
            pl.BlockSpec((dt, d), lambda b, e: (b, 0)),
            pl.BlockSpec((None, None, N_MOD, d), lambda b, e: (layer, cond_row(b, dt), 0, 0)),
            pl.BlockSpec((1, d), lambda b, e: (0, 0)),
        ],
        out_specs=pl.BlockSpec((dt, d), lambda b, e: (b, 0)),
        out_shape=jax.ShapeDtypeStruct((tt, d), F32),
        scratch_shapes=[pltpu.VMEM((dt, d), F32), pltpu.VMEM((de, dt), F32), pltpu.VMEM((de, dt), BF16)],
        compiler_params=_cparams("parallel", "arbitrary"),
        name="peer_dense",
    )(h, u_bf, v_bf, n_tab, e1_tab, r2_tab, e2_tab, x, mods, norm_final.reshape(1, d))


def kernel(x_prompt, x_sample, cache_k, cache_v, c, c_ctx, w_ada, b_ada, norm_mix, norm_ffn, norm_final,
           w_qkv, q_gain, k_gain, w_o, w_bcx, conv_w, conv_b, w_conv_out, w_pq, sub_keys, u_exp, v_exp):
    batch, seq, d = x_prompt.shape
    dec_batch, dec_seq, _ = x_sample.shape
    depth = w_ada.shape[0]
    past = cache_k.shape[2]
    tp = batch * seq
    ts = dec_batch * dec_seq
    tt = tp + ts
    tb = TOKEN_BLOCK
    assert seq % tb == 0 and dec_seq % tb == 0 and dec_batch + 1 <= N_COND_ROWS
    assert tp % DENSE_TOKENS == 0 and dec_seq % DENSE_TOKENS == 0
    assert d == N_HEADS * HEAD_DIM and seq == tb and past == tb and tp % dec_seq == 0
    npb = tp // tb
    bpp = seq // tb
    bps = dec_seq // tb
    nblk = tt // tb
    dq = N_HEADS * HEAD_DIM
    dk = N_KV_HEADS * HEAD_DIM
    n_exp = u_exp.shape[1]
    assert n_exp == N_KEYS * N_KEYS and n_exp % DENSE_CHUNK == 0

    def cond_row(b, rows=tb):
        t0 = b * rows
        return jnp.where(t0 < tp, 0, 1 + (t0 - tp) // dec_seq)

    def mod_spec(layer):
        return pl.BlockSpec((None, None, N_MOD, d), lambda b: (layer, cond_row(b), 0, 0))

    def row_spec(width, dtype_rows=tb):
        return pl.BlockSpec((dtype_rows, width), lambda b: (b, 0))

    def layer_vec(layer, width):
        return pl.BlockSpec((None, 1, width), lambda b: (layer, 0, 0))

    def const2(shape):
        return pl.BlockSpec(shape, lambda b: (0, 0))

    x = jnp.concatenate([x_prompt.reshape(tp, d), x_sample.reshape(ts, d)], axis=0)
    cond = jnp.zeros((N_COND_ROWS, d), F32).at[0].set(c_ctx).at[1:1 + dec_batch].set(c)
    mods = _modulation(cond, w_ada, b_ada).reshape(depth, N_COND_ROWS, N_MOD, d)

    wf_all = _fold_keys(w_pq, sub_keys)
    u_bf = u_exp.astype(BF16)
    v_bf = v_exp.astype(BF16)
    w_qkv_bf = w_qkv.astype(BF16)
    w_o_bf = w_o.astype(BF16)
    w_bcx_bf = w_bcx.astype(BF16)
    w_co_bf = w_conv_out.astype(BF16)
    norm_mix3 = norm_mix.reshape(depth, 1, d)
    norm_ffn3 = norm_ffn.reshape(depth, 1, d)

    cos_t, sin_t = _rope_tables(dec_seq, tb)
    selq, selq_t = _head_selectors(N_HEADS)
    selk, selk_t = _head_selectors(N_KV_HEADS)
    scale = HEAD_DIM ** -0.5 * math.log2(math.e)
    qg_all = jnp.tile(q_gain, (1, N_HEADS)).reshape(-1, 1, dq) * scale
    kg_all = jnp.tile(k_gain, (1, N_KV_HEADS)).reshape(-1, 1, dk)
    ck_all = jnp.transpose(cache_k, (1, 0, 3, 2, 4)).astype(BF16)
    cv_all = jnp.transpose(cache_v, (1, 0, 3, 2, 4)).astype(BF16)

    def rope_block(b):
        return jnp.where(b < npb, 0, 1 + (b - npb) % bps)

    new_k, new_v = [], []
    for i in range(depth):
        j = i // 2
        if i % 2 == 0:
            q, k_hm, v_hm, kf, vf = pl.pallas_call(
                _qkv_kernel,
                grid=(nblk,),
                in_specs=[
                    row_spec(d), mod_spec(i), layer_vec(i, d),
                    pl.BlockSpec((None, d, dq + 2 * dk), lambda b, j=j: (j, 0, 0)),
                    pl.BlockSpec((None, 1, dq), lambda b, j=j: (j, 0, 0)),
                    pl.BlockSpec((None, 1, dk), lambda b, j=j: (j, 0, 0)),
                    const2((dq, 128)), const2((128, dq)), const2((dk, 128)), const2((128, dk)),
                    pl.BlockSpec((tb, dq), lambda b: (rope_block(b), 0)),
                    pl.BlockSpec((tb, dq), lambda b: (rope_block(b), 0)),
                ],
                out_specs=[
                    row_spec(dq),
                    pl.BlockSpec((N_KV_HEADS, tb, HEAD_DIM), lambda b: (0, b, 0)),
                    pl.BlockSpec((N_KV_HEADS, tb, HEAD_DIM), lambda b: (0, b, 0)),
                    row_spec(dk), row_spec(dk),
                ],
                out_shape=[
                    jax.ShapeDtypeStruct((tt, dq), BF16),
                    jax.ShapeDtypeStruct((N_KV_HEADS, tt, HEAD_DIM), BF16),
                    jax.ShapeDtypeStruct((N_KV_HEADS, tt, HEAD_DIM), BF16),
                    jax.ShapeDtypeStruct((tt, dk), F32),
                    jax.ShapeDtypeStruct((tt, dk), F32),
                ],
                compiler_params=_cparams("parallel"),
                name="qkv_project",
            )(x, mods, norm_mix3, w_qkv_bf, qg_all, kg_all, selq, selq_t, selk, selk_t, cos_t, sin_t)
            new_k.append(kf[:tp].reshape(batch, seq, N_KV_HEADS, HEAD_DIM))
            new_v.append(vf[:tp].reshape(batch, seq, N_KV_HEADS, HEAD_DIM))

            def sample_seq(u):
                return jnp.maximum(u - npb, 0) // bps

            kv_a = pl.BlockSpec((None, tb, HEAD_DIM), lambda u, g: (g, jnp.minimum(u, npb - 1), 0))
            kv_b = pl.BlockSpec((None, dec_seq, HEAD_DIM), lambda u, g: (g, tp // dec_seq + sample_seq(u), 0))
            kv_c = pl.BlockSpec((None, None, None, past, HEAD_DIM), lambda u, g, j=j: (j, sample_seq(u), g, 0, 0))
            attn = pl.pallas_call(
                functools.partial(_attn_kernel, n_prompt_blocks=npb),
                grid=(nblk, N_KV_HEADS),
                in_specs=[pl.BlockSpec((tb, dk), lambda u, g: (u, g)), kv_a, kv_a, kv_b, kv_b, kv_c, kv_c],
                out_specs=pl.BlockSpec((tb, dk), lambda u, g: (u, g)),
                out_shape=jax.ShapeDtypeStruct((tt, dq), BF16),
                compiler_params=_cparams("parallel", "parallel"),
                name="attention",
            )(q, k_hm, v_hm, k_hm, v_hm, ck_all, cv_all)

            x = pl.pallas_call(
                _proj_res_kernel,
                grid=(nblk,),
                in_specs=[row_spec(dq), pl.BlockSpec((None, dq, d), lambda b, j=j: (j, 0, 0)), row_spec(d), mod_spec(i)],
                out_specs=row_spec(d),
                out_shape=jax.ShapeDtypeStruct((tt, d), F32),
                compiler_params=_cparams("parallel"),
                name="attn_out_project",
            )(attn, w_o_bf, x, mods)
        else:
            u, gb = pl.pallas_call(
                _conv_in_kernel,
                grid=(nblk,),
                in_specs=[row_spec(d), mod_spec(i), layer_vec(i, d),
                          pl.BlockSpec((None, d, 3 * d), lambda b, j=j: (j, 0, 0))],
                out_specs=[row_spec(d), row_spec(d)],
                out_shape=[jax.ShapeDtypeStruct((tt, d), F32), jax.ShapeDtypeStruct((tt, d), BF16)],
                compiler_params=_cparams("parallel"),
                name="conv_in_project",
            )(x, mods, norm_mix3, w_bcx_bf)
            halo = tb // 8
            x = pl.pallas_call(
                functools.partial(_conv_out_kernel, n_prompt_blocks=npb, blocks_per_prompt_seq=bpp,
                                  blocks_per_sample_seq=bps),
                grid=(nblk,),
                in_specs=[
                    row_spec(d),
                    pl.BlockSpec((8, d), lambda b: (jnp.maximum(b * halo - 1, 0), 0)),
                    pl.BlockSpec((8, d), lambda b: (jnp.minimum((b + 1) * halo, nblk * halo - 1), 0)),
                    row_spec(d),
                    pl.BlockSpec((None, 3, d), lambda b, j=j: (j, 0, 0)),
                    pl.BlockSpec((None, 1, d), lambda b, j=j: (j, 0, 0)),
                    pl.BlockSpec((None, d, d), lambda b, j=j: (j, 0, 0)),
                    row_spec(d), mod_spec(i),
                ],
                out_specs=row_spec(d),
                out_shape=jax.ShapeDtypeStruct((tt, d), F32),
                compiler_params=_cparams("parallel"),
                name="conv_out_project",
            )(u, u, u, gb, conv_w, conv_b.reshape(-1, 1, d), w_co_bf, x, mods)

        h, tabs = _peer_route(x, mods, norm_ffn3, wf_all, i, cond_row)
        x = _peer_dense(h, u_bf, v_bf, tabs, x, mods, i, cond_row, norm_final, final_norm=(i == depth - 1))

    y = x
    y_prompt = y[:tp].reshape(batch, seq, d)
    y_sample = y[tp:].reshape(dec_batch, dec_seq, d)
    return (y_prompt, y_sample, jnp.stack(new_k, axis=1), jnp.stack(new_v, axis=1))
```

```python
import functools
import math

import jax
import jax.numpy as jnp
from jax import lax
from jax.experimental import pallas as pl
from jax.experimental.pallas import tpu as pltpu

F32 = jnp.float32
BF16 = jnp.bfloat16

GRID_W = 64
N_HEADS = 16
N_KV_HEADS = 4
HEAD_DIM = 64
ROT_HALF = 16
ROPE_THETA = 10000.0
PEER_HEADS = 8
N_KEYS = 128
PEER_TOPK = 16
N_MOD = 6
EPS = 1e-6
N_COND_ROWS = 8
SUBLANES = 8

TOKEN_BLOCK = 256
DENSE_TOKENS = 512
DENSE_CHUNK = 2048
MASK_LANES = 256
ROUTE_LANES = 128
VMEM_LIMIT = 52 * 1024 * 1024


def _cparams(*sem):
    return pltpu.CompilerParams(dimension_semantics=sem, vmem_limit_bytes=VMEM_LIMIT)


def _split_bf16(a):
    hi = a.astype(BF16)
    lo = (a - hi.astype(F32)).astype(BF16)
    return hi, lo


def _norm_mod(x, nw, shift, scale):
    ms = jnp.mean(x * x, axis=-1, keepdims=True)
    return (x * lax.rsqrt(ms + EPS)) * nw * (1.0 + scale) + shift


def _gelu_tanh(x):
    c0 = math.sqrt(2.0 / math.pi)
    inner = x * (c0 + (c0 * 0.044715) * (x * x))
    return (0.5 * x) * (1.0 + jnp.tanh(inner))


def _dot_nt(a, b):
    return lax.dot_general(a, b, (((1,), (1,)), ((), ())), preferred_element_type=F32)


def _dot_tn(a, b):
    return lax.dot_general(a, b, (((0,), (0,)), ((), ())), preferred_element_type=F32)


def _mod_kernel(c_ref, w_ref, b_ref, o_ref):
    c = c_ref[...]
    s_hi, s_lo = _split_bf16(jax.nn.silu(c))
    w_hi, w_lo = _split_bf16(w_ref[...])
    acc = jnp.dot(s_hi, w_hi, preferred_element_type=F32)
    acc += jnp.dot(s_hi, w_lo, preferred_element_type=F32)
    acc += jnp.dot(s_lo, w_hi, preferred_element_type=F32)
    o_ref[...] = acc + b_ref[...]


def _modulation(cond, w_ada, b_ada):
    depth, d, nd = w_ada.shape
    nb = 1536
    return pl.pallas_call(
        _mod_kernel,
        grid=(depth, nd // nb),
        in_specs=[
            pl.BlockSpec((N_COND_ROWS, d), lambda l, n: (0, 0)),
            pl.BlockSpec((None, d, nb), lambda l, n: (l, 0, n)),
            pl.BlockSpec((None, 1, nb), lambda l, n: (l, 0, n)),
        ],
        out_specs=pl.BlockSpec((None, N_COND_ROWS, nb), lambda l, n: (l, 0, n)),
        out_shape=jax.ShapeDtypeStruct((depth, N_COND_ROWS, nd), F32),
        compiler_params=_cparams("parallel", "parallel"),
        name="adaln_modulation",
    )(cond, w_ada, b_ada.reshape(depth, 1, nd))


def _fold_kernel(sk_ref, w_ref, o_ref):
    k_hi, k_lo = _split_bf16(sk_ref[...])
    w_hi, w_lo = _split_bf16(w_ref[...])
    acc = _dot_nt(k_hi, w_hi) + _dot_nt(k_hi, w_lo) + _dot_nt(k_lo, w_hi)
    o_ref[...] = acc.astype(BF16)


def _fold_keys(w_pq, sub_keys):
    depth, d, _ = w_pq.shape
    dkh = sub_keys.shape[-1]
    n_parts = PEER_HEADS * 2
    return pl.pallas_call(
        _fold_kernel,
        grid=(depth, n_parts),
        in_specs=[
            pl.BlockSpec((None, None, None, N_KEYS, dkh), lambda l, p: (l, p // 2, p % 2, 0, 0)),
            pl.BlockSpec((None, d, dkh), lambda l, p: (l, 0, p)),
        ],
        out_specs=pl.BlockSpec((None, N_KEYS, d), lambda l, p: (l, p, 0)),
        out_shape=jax.ShapeDtypeStruct((depth, n_parts * N_KEYS, d), BF16),
        compiler_params=_cparams("parallel", "parallel"),
        name="peer_fold_keys",
    )(sub_keys, w_pq)


def _head_rms_scale(q, sel, sel_t):
    hi, lo = _split_bf16(q * q)
    ss = jnp.dot(hi, sel, preferred_element_type=F32) + jnp.dot(lo, sel, preferred_element_type=F32)
    r = lax.rsqrt(ss * (1.0 / HEAD_DIM) + EPS)
    r_hi, r_lo = _split_bf16(r)
    return jnp.dot(r_hi, sel_t, preferred_element_type=F32) + jnp.dot(r_lo, sel_t, preferred_element_type=F32)


def _rope(x, cos, sin_signed):
    n = x.shape[1]
    lane = lax.broadcasted_iota(jnp.int32, x.shape, 1)
    upper = (lane & ROT_HALF) != 0
    partner = jnp.where(upper, pltpu.roll(x, ROT_HALF, axis=1), pltpu.roll(x, n - ROT_HALF, axis=1))
    return x * cos + partner * sin_signed


def _qkv_kernel(x_ref, mod_ref, nw_ref, w_ref, qg_ref, kg_ref, selq_ref, selqt_ref, selk_ref, selkt_ref,
                cos_ref, sin_ref, q_ref, k_ref, v_ref, kf_ref, vf_ref):
    dq = N_HEADS * HEAD_DIM
    dk = N_KV_HEADS * HEAD_DIM
    h = _norm_mod(x_ref[...], nw_ref[...], mod_ref[0:1, :], mod_ref[1:2, :]).astype(BF16)
    qkv = jnp.dot(h, w_ref[...], preferred_element_type=F32)
    q = qkv[:, :dq]
    k = qkv[:, dq:dq + dk]
    v = qkv[:, dq + dk:]
    cos = cos_ref[...]
    sin = sin_ref[...]
    qn = q * _head_rms_scale(q, selq_ref[...], selqt_ref[...]) * qg_ref[...]
    q_ref[...] = _rope(qn, cos, sin).astype(BF16)
    kn = k * _head_rms_scale(k, selk_ref[...], selkt_ref[...]) * kg_ref[...]
    kf_ref[...] = kn
    vf_ref[...] = v
    kr = _rope(kn, cos[:, :dk], sin[:, :dk]).astype(BF16)
    vb = v.astype(BF16)
    for g in range(N_KV_HEADS):
        k_ref[g] = kr[:, g * HEAD_DIM:(g + 1) * HEAD_DIM]
        v_ref[g] = vb[:, g * HEAD_DIM:(g + 1) * HEAD_DIM]


def _softmax_pv(qh, parts):
    scores = [_dot_nt(qh, k) for k, _ in parts]
    m = scores[0].max(axis=-1, keepdims=True)
    for s in scores[1:]:
        m = jnp.maximum(m, s.max(axis=-1, keepdims=True))
    num = None
    den = None
    for s, (_, v) in zip(scores, parts):
        p = jnp.exp2(s - m)
        d = p.sum(axis=-1, keepdims=True)
        o = jnp.dot(p.astype(BF16), v, preferred_element_type=F32)
        num = o if num is None else num + o
        den = d if den is None else den + d
    return num / den


def _attn_kernel(q_ref, ka_ref, va_ref, kb_ref, vb_ref, kc_ref, vc_ref, o_ref, *, n_prompt_blocks):
    u = pl.program_id(0)
    group = N_HEADS // N_KV_HEADS

    @pl.when(u < n_prompt_blocks)
    def _():
        parts = [(ka_ref[...], va_ref[...])]
        for hh in range(group):
            sl = slice(hh * HEAD_DIM, (hh + 1) * HEAD_DIM)
            o_ref[:, sl] = _softmax_pv(q_ref[:, sl], parts).astype(BF16)

    @pl.when(u >= n_prompt_blocks)
    def _():
        parts = [(kb_ref[...], vb_ref[...]), (kc_ref[...], vc_ref[...])]
        for hh in range(group):
            sl = slice(hh * HEAD_DIM, (hh + 1) * HEAD_DIM)
            o_ref[:, sl] = _softmax_pv(q_ref[:, sl], parts).astype(BF16)


def _proj_res_kernel(a_ref, w_ref, x_ref, mod_ref, o_ref):
    o_ref[...] = x_ref[...] + mod_ref[2:3, :] * jnp.dot(a_ref[...], w_ref[...], preferred_element_type=F32)


def _conv_in_kernel(x_ref, mod_ref, nw_ref, w_ref, u_ref, gb_ref):
    d = x_ref.shape[1]
    h = _norm_mod(x_ref[...], nw_ref[...], mod_ref[0:1, :], mod_ref[1:2, :]).astype(BF16)
    bcx = jnp.dot(h, w_ref[...], preferred_element_type=F32)
    gb_ref[...] = bcx[:, :d].astype(BF16)
    u_ref[...] = bcx[:, d:2 * d] * bcx[:, 2 * d:]


def _conv_out_kernel(u_ref, up_ref, un_ref, gb_ref, cw_ref, cb_ref, w_ref, x_ref, mod_ref, o_ref, *,
                     n_prompt_blocks, blocks_per_prompt_seq, blocks_per_sample_seq):
    tb = u_ref.shape[0]
    b = pl.program_id(0)
    bs = b - n_prompt_blocks
    pos = jnp.where(b < n_prompt_blocks, b % blocks_per_prompt_seq, bs % blocks_per_sample_seq)
    per_seq = jnp.where(b < n_prompt_blocks, blocks_per_prompt_seq, blocks_per_sample_seq)
    u = u_ref[...]
    prev_row = jnp.where(pos == 0, 0.0, up_ref[7:8, :])
    next_row = jnp.where(pos == per_seq - 1, 0.0, un_ref[0:1, :])
    row = lax.broadcasted_iota(jnp.int32, u.shape, 0)
    u_m1 = jnp.where(row == 0, prev_row, pltpu.roll(u, 1, axis=0))
    u_p1 = jnp.where(row == tb - 1, next_row, pltpu.roll(u, tb - 1, axis=0))
    conv = u_m1 * cw_ref[0:1, :] + u * cw_ref[1:2, :] + u_p1 * cw_ref[2:3, :] + cb_ref[...]
    y = (gb_ref[...].astype(F32) * conv).astype(BF16)
    o_ref[...] = x_ref[...] + mod_ref[2:3, :] * jnp.dot(y, w_ref[...], preferred_element_type=F32)


def _oddeven_merge(lo, hi, r):
    step = r * 2
    if step < hi - lo:
        yield from _oddeven_merge(lo, hi, step)
        yield from _oddeven_merge(lo + r, hi, step)
        yield from [(i, i + r) for i in range(lo + r, hi - r, step)]
    else:
        yield (lo, lo + r)


def _oddeven_sort(lo, hi):
    if hi - lo >= 1:
        mid = lo + (hi - lo) // 2
        yield from _oddeven_sort(lo, mid)
        yield from _oddeven_sort(mid + 1, hi)
        yield from _oddeven_merge(lo, hi, 1)


def _bitonic_merge_pairs(n):
    out, s = [], n // 2
    while s >= 1:
        out += [(i, i + s) for i in range(n) if (i // s) % 2 == 0]
        s //= 2
    return out


SORT16 = tuple(_oddeven_sort(0, 15))
SORT8 = tuple(_oddeven_sort(0, 7))
BITONIC16 = tuple(_bitonic_merge_pairs(16))


def _apply_net(net, v):
    v = list(v)
    for i, j in net:
        hi, lo = jnp.maximum(v[i], v[j]), jnp.minimum(v[i], v[j])
        v[i], v[j] = hi, lo
    return v


def _merge_top16(a, b):
    return _apply_net(BITONIC16, [jnp.maximum(a[i], b[PEER_TOPK - 1 - i]) for i in range(PEER_TOPK)])


def _sorted_top16(x):
    v = _apply_net(SORT16, [x[SUBLANES * r:SUBLANES * (r + 1)] for r in range(N_KEYS // SUBLANES)])
    for shift in (4, 2, 1):
        v = _merge_top16(v, [pltpu.roll(u, shift, axis=0) for u in v])
    return v


def _pair_threshold(a, b):
    shape = a[0].shape
    neg = jnp.full(shape, -jnp.inf, F32)
    c = {(p, q): a[p - 1] + b[q - 1] for p in range(1, PEER_TOPK + 1) for q in range(1, PEER_TOPK // p + 1)}
    row1 = [c[(1, q)] for q in range(1, 17)]
    col1 = [c[(p, 1)] for p in range(2, 17)] + [neg]
    row2 = [c[(2, q)] for q in range(2, 9)] + [neg]
    col2 = [c[(p, 2)] for p in range(3, 9)] + [neg, neg]
    rest = [c[(3, 3)], c[(3, 4)], c[(3, 5)], c[(4, 3)], c[(5, 3)], c[(4, 4)], neg, neg]
    m2 = _apply_net(BITONIC16, row2 + col2[::-1])
    m3 = _apply_net(SORT8, rest) + [neg] * 8
    top = _merge_top16(_merge_top16(row1, col1), _merge_top16(m2, m3))
    tau = top[PEER_TOPK - 1]
    z = jnp.zeros(shape, F32)
    for v in top:
        z = z + jnp.exp(v - top[0])
    n_of_rank = []
    for p in range(1, PEER_TOPK + 1):
        cnt = jnp.zeros(shape, F32)
        for q in range(1, PEER_TOPK // p + 1):
            cnt = cnt + jnp.where(c[(p, q)] >= tau, 1.0, 0.0)
        n_of_rank.append(cnt)
    return n_of_rank, 1.0 / z


def _key_tables(s1, s2, a, b, n_of_rank, inv_z):
    t = s1.shape[1]
    n_rows, e1_rows, r2_rows, e2_rows = [], [], [], []
    for r in range(N_KEYS // SUBLANES):
        x1 = s1[SUBLANES * r:SUBLANES * (r + 1)]
        x2 = s2[SUBLANES * r:SUBLANES * (r + 1)]
        n = jnp.zeros((SUBLANES, t), F32)
        r2 = jnp.full((SUBLANES, t), float(PEER_TOPK), F32)
        for q in range(PEER_TOPK - 1, -1, -1):
            n = jnp.where(x1 >= a[q], n_of_rank[q], n)
            r2 = jnp.where(x2 >= b[q], float(q), r2)
        n_rows.append(n)
        r2_rows.append(r2)
        e1_rows.append(jnp.exp(x1 - a[0]) * inv_z)
        e2_rows.append(jnp.exp(x2 - b[0]))
    cat = lambda rows: jnp.concatenate(rows, axis=0)
    return cat(n_rows), cat(e1_rows), cat(r2_rows), cat(e2_rows)


def _row_to_all_rows(x, row):
    sub = lax.broadcasted_iota(jnp.int32, x.shape, 0)
    y = jnp.where(sub == row, x, 0.0)
    for shift in (4, 2, 1):
        y = y + pltpu.roll(y, shift, axis=0)
    return y


def _route_kernel(x_ref, mod_ref, nw_ref, wf_ref, h_ref, n_ref, e1_ref, r2_ref, e2_ref,
                  s_ref, lists_ref, heads_ref, stats_ref):
    h = _norm_mod(x_ref[...], nw_ref[...], mod_ref[3:4, :], mod_ref[4:5, :]).astype(BF16)
    h_ref[...] = h
    s_ref[...] = _dot_nt(wf_ref[...], h)
    t = s_ref.shape[1]
    k2 = 2 * PEER_TOPK
    lane_groups = [slice(l * ROUTE_LANES, (l + 1) * ROUTE_LANES) for l in range(t // ROUTE_LANES)]
    heads_ref[...] = jnp.zeros_like(heads_ref)

    def half_scores(hd, ls):
        r1 = pl.multiple_of(hd * 2 * N_KEYS, 2 * N_KEYS)
        return s_ref[pl.ds(r1, N_KEYS), ls], s_ref[pl.ds(r1 + N_KEYS, N_KEYS), ls]

    def sort_head(hd, carry):
        for ls in lane_groups:
            s1, s2 = half_scores(hd, ls)
            ab = _sorted_top16(s1) + _sorted_top16(s2)
            sub = lax.broadcasted_iota(jnp.int32, ab[0].shape, 0)
            for q in range(k2):
                lists_ref[hd, q, :, ls] = ab[q]
                heads_ref[q, :, ls] = jnp.where(sub == hd, ab[q], heads_ref[q, :, ls])
        return carry

    lax.fori_loop(0, PEER_HEADS, sort_head, 0)

    for ls in lane_groups:
        ab = [heads_ref[q, :, ls] for q in range(k2)]
        n_of_rank, inv_z = _pair_threshold(ab[:PEER_TOPK], ab[PEER_TOPK:])
        for p in range(PEER_TOPK):
            stats_ref[p, :, ls] = n_of_rank[p]
        stats_ref[PEER_TOPK, :, ls] = inv_z

    def tables_head(hd, carry):
        for ls in lane_groups:
            s1, s2 = half_scores(hd, ls)
            ab = [lists_ref[hd, q, :, ls] for q in range(k2)]
            stats = [_row_to_all_rows(stats_ref[p, :, ls], hd) for p in range(PEER_TOPK + 1)]
            n, e1, r2, e2 = _key_tables(s1, s2, ab[:PEER_TOPK], ab[PEER_TOPK:], stats[:PEER_TOPK], stats[PEER_TOPK])
            n_ref[hd, :, ls] = n
            e1_ref[hd, :, ls] = e1
            r2_ref[hd, :, ls] = r2.astype(BF16)
            e2_ref[hd, :, ls] = e2.astype(BF16)
        return carry

    lax.fori_loop(0, PEER_HEADS, tables_head, 0)


def _dense_kernel(h_ref, u_ref, v_ref, n_ref, e1_ref, r2_ref, e2_ref, x_ref, mod_ref, nf_ref, o_ref,
                  acc_ref, a_ref, p_ref, *, final_norm):
    c = pl.program_id(1)
    n_rows = u_ref.shape[0] // N_KEYS
    tb = h_ref.shape[0]
    sub = SUBLANES

    @pl.when(c == 0)
    def _():
        acc_ref[...] = jnp.zeros_like(acc_ref)

    a_ref[...] = _dot_nt(u_ref[...], h_ref[...])

    for l in range(tb // MASK_LANES):
        ls = slice(l * MASK_LANES, (l + 1) * MASK_LANES)
        for ii in range(n_rows):
            rs = slice(ii * N_KEYS, (ii + 1) * N_KEYS)
            w = jnp.zeros((N_KEYS // sub, sub, MASK_LANES), BF16)
            for hd in range(PEER_HEADS):
                nb = jnp.broadcast_to(n_ref[hd, ii:ii + 1, ls], (sub, MASK_LANES)).astype(BF16)
                eb = jnp.broadcast_to(e1_ref[hd, ii:ii + 1, ls], (sub, MASK_LANES)).astype(BF16)
                r2 = r2_ref[hd, :, ls].reshape(N_KEYS // sub, sub, MASK_LANES)
                e2 = e2_ref[hd, :, ls].reshape(N_KEYS // sub, sub, MASK_LANES)
                w = w + jnp.where(r2 < nb[None], e2 * eb[None], jnp.zeros_like(e2))
            p_ref[rs, ls] = _gelu_tanh(a_ref[rs, ls].astype(BF16)) * w.reshape(N_KEYS, MASK_LANES)
    acc_ref[...] += _dot_tn(p_ref[...], v_ref[...])

    @pl.when(c == pl.num_programs(1) - 1)
    def _():
        y = x_ref[...] + mod_ref[5:6, :] * acc_ref[...]
        if final_norm:
            ms = jnp.mean(y * y, axis=-1, keepdims=True)
            y = y * lax.rsqrt(ms + EPS) * nf_ref[...]
        o_ref[...] = y


def _rope_tables(seq_len, n_identity):
    axis_dim = HEAD_DIM // 2
    n_rows = seq_len // GRID_W
    row = jnp.repeat(jnp.arange(n_rows, dtype=F32), GRID_W)
    col = jnp.tile(jnp.arange(GRID_W, dtype=F32), n_rows)
    inv = ROPE_THETA ** (-jnp.arange(0, axis_dim, 2, dtype=F32) / axis_dim)
    ang_r = row[:, None] * inv
    ang_c = col[:, None] * inv
    cos = jnp.concatenate([jnp.cos(ang_r)] * 2 + [jnp.cos(ang_c)] * 2, axis=-1)
    sin = jnp.concatenate([-jnp.sin(ang_r), jnp.sin(ang_r), -jnp.sin(ang_c), jnp.sin(ang_c)], axis=-1)
    cos = jnp.concatenate([jnp.ones((n_identity, HEAD_DIM), F32), cos], axis=0)
    sin = jnp.concatenate([jnp.zeros((n_identity, HEAD_DIM), F32), sin], axis=0)
    return jnp.tile(cos, (1, N_HEADS)), jnp.tile(sin, (1, N_HEADS))


def _head_selectors(n_heads):
    c = jnp.arange(n_heads * HEAD_DIM)[:, None] // HEAD_DIM
    sel = (c == jnp.arange(128)[None, :]).astype(BF16)
    return sel, sel.T


def _peer_route(x, mods, norm_ffn3, wf_all, layer, cond_row):
    tt, d = x.shape
    tb = TOKEN_BLOCK
    n_keys2 = PEER_HEADS * 2 * N_KEYS
    tab = lambda dtype: jax.ShapeDtypeStruct((PEER_HEADS, N_KEYS, tt), dtype)
    tab_spec = pl.BlockSpec((PEER_HEADS, N_KEYS, tb), lambda b: (0, 0, b))
    row_spec = pl.BlockSpec((tb, d), lambda b: (b, 0))
    h, *tabs = pl.pallas_call(
        _route_kernel,
        grid=(tt // tb,),
        in_specs=[row_spec,
                  pl.BlockSpec((None, None, N_MOD, d), lambda b: (layer, cond_row(b, tb), 0, 0)),
                  pl.BlockSpec((None, 1, d), lambda b: (layer, 0, 0)),
                  pl.BlockSpec((None, n_keys2, d), lambda b: (layer, 0, 0))],
        out_specs=[row_spec, tab_spec, tab_spec, tab_spec, tab_spec],
        out_shape=[jax.ShapeDtypeStruct((tt, d), BF16), tab(F32), tab(F32), tab(BF16), tab(BF16)],
        scratch_shapes=[pltpu.VMEM((n_keys2, tb), F32),
                        pltpu.VMEM((PEER_HEADS, 2 * PEER_TOPK, SUBLANES, tb), F32),
                        pltpu.VMEM((2 * PEER_TOPK, SUBLANES, tb), F32),
                        pltpu.VMEM((PEER_TOPK + 1, SUBLANES, tb), F32)],
        compiler_params=_cparams("parallel"),
        name="peer_route",
    )(x, mods, norm_ffn3, wf_all)
    return h, tabs


def _peer_dense(h, u_bf, v_bf, tabs, x, mods, layer, cond_row, norm_final, final_norm):
    tt, d = x.shape
    n_exp = u_bf.shape[1]
    dt, de = DENSE_TOKENS, DENSE_CHUNK
    n_tab, e1_tab, r2_tab, e2_tab = tabs
    dtab = pl.BlockSpec((PEER_HEADS, N_KEYS, dt), lambda b, e: (0, 0, b))
    rtab = pl.BlockSpec((PEER_HEADS, de // N_KEYS, dt), lambda b, e: (0, e, b))
    return pl.pallas_call(
        functools.partial(_dense_kernel, final_norm=final_norm),
        grid=(tt // dt, n_exp // de),
        in_specs=[
            pl.BlockSpec((dt, d), lambda b, e: (b, 0)),
            pl.BlockSpec((None, de, d), lambda b, e: (layer, e, 0)),
            pl.BlockSpec((None, de, d), lambda b, e: (layer, e, 0)),
            rtab, rtab, dtab, dtab,
            pl.BlockSpec((dt, d), lambda b, e: (b, 0)),
            pl.BlockSpec((None, None, N_MOD, d), lambda b, e: (layer, cond_row(b, dt), 0, 0)),
            pl.BlockSpec((1, d), lambda b, e: (0, 0)),
        ],
        out_specs=pl.BlockSpec((dt, d), lambda b, e: (b, 0)),
        out_shape=jax.ShapeDtypeStruct((tt, d), F32),
        scratch_shapes=[pltpu.VMEM((dt, d), F32), pltpu.VMEM((de, dt), F32), pltpu.VMEM((de, dt), BF16)],
        compiler_params=_cparams("parallel", "arbitrary"),
        name="peer_dense",
    )(h, u_bf, v_bf, n_tab, e1_tab, r2_tab, e2_tab, x, mods, norm_final.reshape(1, d))


def kernel(x_prompt, x_sample, cache_k, cache_v, c, c_ctx, w_ada, b_ada, norm_mix, norm_ffn, norm_final,
           w_qkv, q_gain, k_gain, w_o, w_bcx, conv_w, conv_b, w_conv_out, w_pq, sub_keys, u_exp, v_exp):
    batch, seq, d = x_prompt.shape
    dec_batch, dec_seq, _ = x_sample.shape
    depth = w_ada.shape[0]
    past = cache_k.shape[2]
    tp = batch * seq
    ts = dec_batch * dec_seq
    tt = tp + ts
    tb = TOKEN_BLOCK
    assert seq % tb == 0 and dec_seq % tb == 0 and dec_batch + 1 <= N_COND_ROWS
    assert tp % DENSE_TOKENS == 0 and dec_seq % DENSE_TOKENS == 0
    assert d == N_HEADS * HEAD_DIM and seq == tb and past == tb and tp % dec_seq == 0
    npb = tp // tb
    bpp = seq // tb
    bps = dec_seq // tb
    nblk = tt // tb
    dq = N_HEADS * HEAD_DIM
    dk = N_KV_HEADS * HEAD_DIM
    n_exp = u_exp.shape[1]
    assert n_exp == N_KEYS * N_KEYS and n_exp % DENSE_CHUNK == 0

    def cond_row(b, rows=tb):
        t0 = b * rows
        return jnp.where(t0 < tp, 0, 1 + (t0 - tp) // dec_seq)

    def mod_spec(layer):
        return pl.BlockSpec((None, None, N_MOD, d), lambda b: (layer, cond_row(b), 0, 0))

    def row_spec(width, dtype_rows=tb):
        return pl.BlockSpec((dtype_rows, width), lambda b: (b, 0))

    def layer_vec(layer, width):
        return pl.BlockSpec((None, 1, width), lambda b: (layer, 0, 0))

    def const2(shape):
        return pl.BlockSpec(shape, lambda b: (0, 0))

    x = jnp.concatenate([x_prompt.reshape(tp, d), x_sample.reshape(ts, d)], axis=0)
    cond = jnp.zeros((N_COND_ROWS, d), F32).at[0].set(c_ctx).at[1:1 + dec_batch].set(c)
    mods = _modulation(cond, w_ada, b_ada).reshape(depth, N_COND_ROWS, N_MOD, d)

    wf_all = _fold_keys(w_pq, sub_keys)
    u_bf = u_exp.astype(BF16)
    v_bf = v_exp.astype(BF16)
    w_qkv_bf = w_qkv.astype(BF16)
    w_o_bf = w_o.astype(BF16)
    w_bcx_bf = w_bcx.astype(BF16)
    w_co_bf = w_conv_out.astype(BF16)
    norm_mix3 = norm_mix.reshape(depth, 1, d)
    norm_ffn3 = norm_ffn.reshape(depth, 1, d)

    cos_t, sin_t = _rope_tables(dec_seq, tb)
    selq, selq_t = _head_selectors(N_HEADS)
    selk, selk_t = _head_selectors(N_KV_HEADS)
    scale = HEAD_DIM ** -0.5 * math.log2(math.e)
    qg_all = jnp.tile(q_gain, (1, N_HEADS)).reshape(-1, 1, dq) * scale
    kg_all = jnp.tile(k_gain, (1, N_KV_HEADS)).reshape(-1, 1, dk)
    ck_all = jnp.transpose(cache_k, (1, 0, 3, 2, 4)).astype(BF16)
    cv_all = jnp.transpose(cache_v, (1, 0, 3, 2, 4)).astype(BF16)

    def rope_block(b):
        return jnp.where(b < npb, 0, 1 + (b - npb) % bps)

    new_k, new_v = [], []
    for i in range(depth):
        j = i // 2
        if i % 2 == 0:
            q, k_hm, v_hm, kf, vf = pl.pallas_call(
                _qkv_kernel,
                grid=(nblk,),
                in_specs=[
                    row_spec(d), mod_spec(i), layer_vec(i, d),
                    pl.BlockSpec((None, d, dq + 2 * dk), lambda b, j=j: (j, 0, 0)),
                    pl.BlockSpec((None, 1, dq), lambda b, j=j: (j, 0, 0)),
                    pl.BlockSpec((None, 1, dk), lambda b, j=j: (j, 0, 0)),
                    const2((dq, 128)), const2((128, dq)), const2((dk, 128)), const2((128, dk)),
                    pl.BlockSpec((tb, dq), lambda b: (rope_block(b), 0)),
                    pl.BlockSpec((tb, dq), lambda b: (rope_block(b), 0)),
                ],
                out_specs=[
                    row_spec(dq),
                    pl.BlockSpec((N_KV_HEADS, tb, HEAD_DIM), lambda b: (0, b, 0)),
                    pl.BlockSpec((N_KV_HEADS, tb, HEAD_DIM), lambda b: (0, b, 0)),
                    row_spec(dk), row_spec(dk),
                ],
                out_shape=[
                    jax.ShapeDtypeStruct((tt, dq), BF16),
                    jax.ShapeDtypeStruct((N_KV_HEADS, tt, HEAD_DIM), BF16),
                    jax.ShapeDtypeStruct((N_KV_HEADS, tt, HEAD_DIM), BF16),
                    jax.ShapeDtypeStruct((tt, dk), F32),
                    jax.ShapeDtypeStruct((tt, dk), F32),
                ],
                compiler_params=_cparams("parallel"),
                name="qkv_project",
            )(x, mods, norm_mix3, w_qkv_bf, qg_all, kg_all, selq, selq_t, selk, selk_t, cos_t, sin_t)
            new_k.append(kf[:tp].reshape(batch, seq, N_KV_HEADS, HEAD_DIM))
            new_v.append(vf[:tp].reshape(batch, seq, N_KV_HEADS, HEAD_DIM))

            def sample_seq(u):
                return jnp.maximum(u - npb, 0) // bps

            kv_a = pl.BlockSpec((None, tb, HEAD_DIM), lambda u, g: (g, jnp.minimum(u, npb - 1), 0))
            kv_b = pl.BlockSpec((None, dec_seq, HEAD_DIM), lambda u, g: (g, tp // dec_seq + sample_seq(u), 0))
            kv_c = pl.BlockSpec((None, None, None, past, HEAD_DIM), lambda u, g, j=j: (j, sample_seq(u), g, 0, 0))
            attn = pl.pallas_call(
                functools.partial(_attn_kernel, n_prompt_blocks=npb),
                grid=(nblk, N_KV_HEADS),
                in_specs=[pl.BlockSpec((tb, dk), lambda u, g: (u, g)), kv_a, kv_a, kv_b, kv_b, kv_c, kv_c],
                out_specs=pl.BlockSpec((tb, dk), lambda u, g: (u, g)),
                out_shape=jax.ShapeDtypeStruct((tt, dq), BF16),
                compiler_params=_cparams("parallel", "parallel"),
                name="attention",
            )(q, k_hm, v_hm, k_hm, v_hm, ck_all, cv_all)

            x = pl.pallas_call(
                _proj_res_kernel,
                grid=(nblk,),
                in_specs=[row_spec(dq), pl.BlockSpec((None, dq, d), lambda b, j=j: (j, 0, 0)), row_spec(d), mod_spec(i)],
                out_specs=row_spec(d),
                out_shape=jax.ShapeDtypeStruct((tt, d), F32),
                compiler_params=_cparams("parallel"),
                name="attn_out_project",
            )(attn, w_o_bf, x, mods)
        else:
            u, gb = pl.pallas_call(
                _conv_in_kernel,
                grid=(nblk,),
                in_specs=[row_spec(d), mod_spec(i), layer_vec(i, d),
                          pl.BlockSpec((None, d, 3 * d), lambda b, j=j: (j, 0, 0))],
                out_specs=[row_spec(d), row_spec(d)],
                out_shape=[jax.ShapeDtypeStruct((tt, d), F32), jax.ShapeDtypeStruct((tt, d), BF16)],
                compiler_params=_cparams("parallel"),
                name="conv_in_project",
            )(x, mods, norm_mix3, w_bcx_bf)
            halo = tb // 8
            x = pl.pallas_call(
                functools.partial(_conv_out_kernel, n_prompt_blocks=npb, blocks_per_prompt_seq=bpp,
                                  blocks_per_sample_seq=bps),
                grid=(nblk,),
                in_specs=[
                    row_spec(d),
                    pl.BlockSpec((8, d), lambda b: (jnp.maximum(b * halo - 1, 0), 0)),
                    pl.BlockSpec((8, d), lambda b: (jnp.minimum((b + 1) * halo, nblk * halo - 1), 0)),
                    row_spec(d),
                    pl.BlockSpec((None, 3, d), lambda b, j=j: (j, 0, 0)),
                    pl.BlockSpec((None, 1, d), lambda b, j=j: (j, 0, 0)),
                    pl.BlockSpec((None, d, d), lambda b, j=j: (j, 0, 0)),
                    row_spec(d), mod_spec(i),
                ],
                out_specs=row_spec(d),
                out_shape=jax.ShapeDtypeStruct((tt, d), F32),
                compiler_params=_cparams("parallel"),
                name="conv_out_project",
            )(u, u, u, gb, conv_w, conv_b.reshape(-1, 1, d), w_co_bf, x, mods)

        h, tabs = _peer_route(x, mods, norm_ffn3, wf_all, i, cond_row)
        x = _peer_dense(h, u_bf, v_bf, tabs, x, mods, i, cond_row, norm_final, final_norm=(i == depth - 1))

    y = x
    y_prompt = y[:tp].reshape(batch, seq, d)
    y_sample = y[tp:].reshape(dec_batch, dec_seq, d)
    return (y_prompt, y_sample, jnp.stack(new_k, axis=1), jnp.stack(new_v, axis=1))
```

```python
import functools
import math

import jax
import jax.numpy as jnp
from jax import lax
from jax.experimental import pallas as pl
from jax.experimental.pallas import tpu as pltpu

F32 = jnp.float32
BF16 = jnp.bfloat16

GRID_W = 64
N_HEADS = 16
N_KV_HEADS = 4
HEAD_DIM = 64
ROT_HALF = 16
ROPE_THETA = 10000.0
PEER_HEADS = 8
N_KEYS = 128
PEER_TOPK = 16
N_MOD = 6
EPS = 1e-6
N_COND_ROWS = 8
SUBLANES = 8

TOKEN_BLOCK = 256
DENSE_TOKENS = 512
DENSE_CHUNK = 2048
DENSE_X_ROWS = 256
DENSE_Z_ROWS = 512
MASK_LANES = 256
ROUTE_LANES = 128
VMEM_LIMIT = 52 * 1024 * 1024


def _cparams(*sem):
    return pltpu.CompilerParams(dimension_semantics=sem, vmem_limit_bytes=VMEM_LIMIT)


def _split_bf16(a):
    hi = a.astype(BF16)
    lo = (a - hi.astype(F32)).astype(BF16)
    return hi, lo


def _norm_mod(x, nw, shift, scale):
    ms = jnp.mean(x * x, axis=-1, keepdims=True)
    return (x * lax.rsqrt(ms + EPS)) * nw * (1.0 + scale) + shift


def _gelu_tanh(x):
    c0 = math.sqrt(2.0 / math.pi)
    inner = x * (c0 + (c0 * 0.044715) * (x * x))
    return (0.5 * x) * (1.0 + jnp.tanh(inner))


def _dot_nt(a, b):
    return lax.dot_general(a, b, (((1,), (1,)), ((), ())), preferred_element_type=F32)


def _mod_kernel(c_ref, w_ref, b_ref, o_ref):
    c = c_ref[...]
    s_hi, s_lo = _split_bf16(jax.nn.silu(c))
    w_hi, w_lo = _split_bf16(w_ref[...])
    acc = jnp.dot(s_hi, w_hi, preferred_element_type=F32)
    acc += jnp.dot(s_hi, w_lo, preferred_element_type=F32)
    acc += jnp.dot(s_lo, w_hi, preferred_element_type=F32)
    o_ref[...] = acc + b_ref[...]


def _modulation(cond, w_ada, b_ada):
    depth, d, nd = w_ada.shape
    nb = 1536
    return pl.pallas_call(
        _mod_kernel,
        grid=(depth, nd // nb),
        in_specs=[
            pl.BlockSpec((N_COND_ROWS, d), lambda l, n: (0, 0)),
            pl.BlockSpec((None, d, nb), lambda l, n: (l, 0, n)),
            pl.BlockSpec((None, 1, nb), lambda l, n: (l, 0, n)),
        ],
        out_specs=pl.BlockSpec((None, N_COND_ROWS, nb), lambda l, n: (l, 0, n)),
        out_shape=jax.ShapeDtypeStruct((depth, N_COND_ROWS, nd), F32),
        compiler_params=_cparams("parallel", "parallel"),
        name="adaln_modulation",
    )(cond, w_ada, b_ada.reshape(depth, 1, nd))


def _fold_kernel(sk_ref, w_ref, o_ref):
    k_hi, k_lo = _split_bf16(sk_ref[...])
    w_hi, w_lo = _split_bf16(w_ref[...])
    acc = _dot_nt(k_hi, w_hi) + _dot_nt(k_hi, w_lo) + _dot_nt(k_lo, w_hi)
    o_ref[...] = acc.astype(BF16)


def _fold_keys(w_pq, sub_keys):
    depth, d, _ = w_pq.shape
    dkh = sub_keys.shape[-1]
    n_parts = PEER_HEADS * 2
    return pl.pallas_call(
        _fold_kernel,
        grid=(depth, n_parts),
        in_specs=[
            pl.BlockSpec((None, None, None, N_KEYS, dkh), lambda l, p: (l, p // 2, p % 2, 0, 0)),
            pl.BlockSpec((None, d, dkh), lambda l, p: (l, 0, p)),
        ],
        out_specs=pl.BlockSpec((None, N_KEYS, d), lambda l, p: (l, p, 0)),
        out_shape=jax.ShapeDtypeStruct((depth, n_parts * N_KEYS, d), BF16),
        compiler_params=_cparams("parallel", "parallel"),
        name="peer_fold_keys",
    )(sub_keys, w_pq)


def _head_rms_scale(q, sel, sel_t):
    hi, lo = _split_bf16(q * q)
    ss = jnp.dot(hi, sel, preferred_element_type=F32) + jnp.dot(lo, sel, preferred_element_type=F32)
    r = lax.rsqrt(ss * (1.0 / HEAD_DIM) + EPS)
    r_hi, r_lo = _split_bf16(r)
    return jnp.dot(r_hi, sel_t, preferred_element_type=F32) + jnp.dot(r_lo, sel_t, preferred_element_type=F32)


def _rope(x, cos, sin_signed):
    n = x.shape[1]
    lane = lax.broadcasted_iota(jnp.int32, x.shape, 1)
    upper = (lane & ROT_HALF) != 0
    partner = jnp.where(upper, pltpu.roll(x, ROT_HALF, axis=1), pltpu.roll(x, n - ROT_HALF, axis=1))
    return x * cos + partner * sin_signed


def _qkv_kernel(x_ref, mod_ref, nw_ref, w_ref, qg_ref, kg_ref, selq_ref, selqt_ref, selk_ref, selkt_ref,
                cos_ref, sin_ref, q_ref, k_ref, v_ref, kf_ref, vf_ref):
    dq = N_HEADS * HEAD_DIM
    dk = N_KV_HEADS * HEAD_DIM
    h = _norm_mod(x_ref[...], nw_ref[...], mod_ref[0:1, :], mod_ref[1:2, :]).astype(BF16)
    qkv = jnp.dot(h, w_ref[...], preferred_element_type=F32)
    q = qkv[:, :dq]
    k = qkv[:, dq:dq + dk]
    v = qkv[:, dq + dk:]
    cos = cos_ref[...]
    sin = sin_ref[...]
    qn = q * _head_rms_scale(q, selq_ref[...], selqt_ref[...]) * qg_ref[...]
    q_ref[...] = _rope(qn, cos, sin).astype(BF16)
    kn = k * _head_rms_scale(k, selk_ref[...], selkt_ref[...]) * kg_ref[...]
    kf_ref[...] = kn
    vf_ref[...] = v
    kr = _rope(kn, cos[:, :dk], sin[:, :dk]).astype(BF16)
    vb = v.astype(BF16)
    for g in range(N_KV_HEADS):
        k_ref[g] = kr[:, g * HEAD_DIM:(g + 1) * HEAD_DIM]
        v_ref[g] = vb[:, g * HEAD_DIM:(g + 1) * HEAD_DIM]


def _softmax_pv(qh, parts):
    scores = [_dot_nt(qh, k) for k, _ in parts]
    m = scores[0].max(axis=-1, keepdims=True)
    for s in scores[1:]:
        m = jnp.maximum(m, s.max(axis=-1, keepdims=True))
    num = None
    den = None
    for s, (_, v) in zip(scores, parts):
        p = jnp.exp2(s - m)
        d = p.sum(axis=-1, keepdims=True)
        o = jnp.dot(p.astype(BF16), v, preferred_element_type=F32)
        num = o if num is None else num + o
        den = d if den is None else den + d
    return num / den


def _attn_kernel(q_ref, ka_ref, va_ref, kb_ref, vb_ref, kc_ref, vc_ref, o_ref, *, n_prompt_blocks):
    u = pl.program_id(0)
    group = N_HEADS // N_KV_HEADS

    @pl.when(u < n_prompt_blocks)
    def _():
        parts = [(ka_ref[...], va_ref[...])]
        for hh in range(group):
            sl = slice(hh * HEAD_DIM, (hh + 1) * HEAD_DIM)
            o_ref[:, sl] = _softmax_pv(q_ref[:, sl], parts).astype(BF16)

    @pl.when(u >= n_prompt_blocks)
    def _():
        parts = [(kb_ref[...], vb_ref[...]), (kc_ref[...], vc_ref[...])]
        for hh in range(group):
            sl = slice(hh * HEAD_DIM, (hh + 1) * HEAD_DIM)
            o_ref[:, sl] = _softmax_pv(q_ref[:, sl], parts).astype(BF16)


def _proj_res_kernel(a_ref, w_ref, x_ref, mod_ref, o_ref):
    o_ref[...] = x_ref[...] + mod_ref[2:3, :] * jnp.dot(a_ref[...], w_ref[...], preferred_element_type=F32)


def _conv_in_kernel(x_ref, mod_ref, nw_ref, w_ref, u_ref, gb_ref):
    d = x_ref.shape[1]
    h = _norm_mod(x_ref[...], nw_ref[...], mod_ref[0:1, :], mod_ref[1:2, :]).astype(BF16)
    bcx = jnp.dot(h, w_ref[...], preferred_element_type=F32)
    gb_ref[...] = bcx[:, :d].astype(BF16)
    u_ref[...] = bcx[:, d:2 * d] * bcx[:, 2 * d:]


def _conv_out_kernel(u_ref, up_ref, un_ref, gb_ref, cw_ref, cb_ref, w_ref, x_ref, mod_ref, o_ref, *,
                     n_prompt_blocks, blocks_per_prompt_seq, blocks_per_sample_seq):
    tb = u_ref.shape[0]
    b = pl.program_id(0)
    bs = b - n_prompt_blocks
    pos = jnp.where(b < n_prompt_blocks, b % blocks_per_prompt_seq, bs % blocks_per_sample_seq)
    per_seq = jnp.where(b < n_prompt_blocks, blocks_per_prompt_seq, blocks_per_sample_seq)
    u = u_ref[...]
    prev_row = jnp.where(pos == 0, 0.0, up_ref[7:8, :])
    next_row = jnp.where(pos == per_seq - 1, 0.0, un_ref[0:1, :])
    row = lax.broadcasted_iota(jnp.int32, u.shape, 0)
    u_m1 = jnp.where(row == 0, prev_row, pltpu.roll(u, 1, axis=0))
    u_p1 = jnp.where(row == tb - 1, next_row, pltpu.roll(u, tb - 1, axis=0))
    conv = u_m1 * cw_ref[0:1, :] + u * cw_ref[1:2, :] + u_p1 * cw_ref[2:3, :] + cb_ref[...]
    y = (gb_ref[...].astype(F32) * conv).astype(BF16)
    o_ref[...] = x_ref[...] + mod_ref[2:3, :] * jnp.dot(y, w_ref[...], preferred_element_type=F32)


def _oddeven_merge(lo, hi, r):
    step = r * 2
    if step < hi - lo:
        yield from _oddeven_merge(lo, hi, step)
        yield from _oddeven_merge(lo + r, hi, step)
        yield from [(i, i + r) for i in range(lo + r, hi - r, step)]
    else:
        yield (lo, lo + r)


def _oddeven_sort(lo, hi):
    if hi - lo >= 1:
        mid = lo + (hi - lo) // 2
        yield from _oddeven_sort(lo, mid)
        yield from _oddeven_sort(mid + 1, hi)
        yield from _oddeven_merge(lo, hi, 1)


def _bitonic_merge_pairs(n):
    out, s = [], n // 2
    while s >= 1:
        out += [(i, i + s) for i in range(n) if (i // s) % 2 == 0]
        s //= 2
    return out


SORT16 = tuple(_oddeven_sort(0, 15))
SORT8 = tuple(_oddeven_sort(0, 7))
BITONIC16 = tuple(_bitonic_merge_pairs(16))


def _apply_net(net, v):
    v = list(v)
    for i, j in net:
        hi, lo = jnp.maximum(v[i], v[j]), jnp.minimum(v[i], v[j])
        v[i], v[j] = hi, lo
    return v


def _merge_top16(a, b):
    return _apply_net(BITONIC16, [jnp.maximum(a[i], b[PEER_TOPK - 1 - i]) for i in range(PEER_TOPK)])


def _sorted_top16(x):
    v = _apply_net(SORT16, [x[SUBLANES * r:SUBLANES * (r + 1)] for r in range(N_KEYS // SUBLANES)])
    for shift in (4, 2, 1):
        v = _merge_top16(v, [pltpu.roll(u, shift, axis=0) for u in v])
    return v


def _pair_threshold(a, b):
    shape = a[0].shape
    neg = jnp.full(shape, -jnp.inf, F32)
    c = {(p, q): a[p - 1] + b[q - 1] for p in range(1, PEER_TOPK + 1) for q in range(1, PEER_TOPK // p + 1)}
    row1 = [c[(1, q)] for q in range(1, 17)]
    col1 = [c[(p, 1)] for p in range(2, 17)] + [neg]
    row2 = [c[(2, q)] for q in range(2, 9)] + [neg]
    col2 = [c[(p, 2)] for p in range(3, 9)] + [neg, neg]
    rest = [c[(3, 3)], c[(3, 4)], c[(3, 5)], c[(4, 3)], c[(5, 3)], c[(4, 4)], neg, neg]
    m2 = _apply_net(BITONIC16, row2 + col2[::-1])
    m3 = _apply_net(SORT8, rest) + [neg] * 8
    top = _merge_top16(_merge_top16(row1, col1), _merge_top16(m2, m3))
    tau = top[PEER_TOPK - 1]
    z = jnp.zeros(shape, F32)
    for v in top:
        z = z + jnp.exp(v - top[0])
    n_of_rank = []
    for p in range(1, PEER_TOPK + 1):
        cnt = jnp.zeros(shape, F32)
        for q in range(1, PEER_TOPK // p + 1):
            cnt = cnt + jnp.where(c[(p, q)] >= tau, 1.0, 0.0)
        n_of_rank.append(cnt)
    return n_of_rank, 1.0 / z


def _key_tables(s1, s2, a, b, n_of_rank, inv_z):
    t = s1.shape[1]
    n_rows, e1_rows, r2_rows, e2_rows = [], [], [], []
    for r in range(N_KEYS // SUBLANES):
        x1 = s1[SUBLANES * r:SUBLANES * (r + 1)]
        x2 = s2[SUBLANES * r:SUBLANES * (r + 1)]
        n = jnp.zeros((SUBLANES, t), F32)
        r2 = jnp.full((SUBLANES, t), float(PEER_TOPK), F32)
        for q in range(PEER_TOPK - 1, -1, -1):
            n = jnp.where(x1 >= a[q], n_of_rank[q], n)
            r2 = jnp.where(x2 >= b[q], float(q), r2)
        n_rows.append(n)
        r2_rows.append(r2)
        e1_rows.append(jnp.exp(x1 - a[0]) * inv_z)
        e2_rows.append(jnp.exp(x2 - b[0]))
    cat = lambda rows: jnp.concatenate(rows, axis=0)
    return cat(n_rows), cat(e1_rows), cat(r2_rows), cat(e2_rows)


def _row_to_all_rows(x, row):
    sub = lax.broadcasted_iota(jnp.int32, x.shape, 0)
    y = jnp.where(sub == row, x, 0.0)
    for shift in (4, 2, 1):
        y = y + pltpu.roll(y, shift, axis=0)
    return y


def _route_kernel(x_ref, mod_ref, nw_ref, wf_ref, h_ref, n_ref, e1_ref, r2_ref, e2_ref,
                  s_ref, lists_ref, heads_ref, stats_ref):
    h = _norm_mod(x_ref[...], nw_ref[...], mod_ref[3:4, :], mod_ref[4:5, :]).astype(BF16)
    h_ref[...] = h
    s_ref[...] = _dot_nt(wf_ref[...], h)
    t = s_ref.shape[1]
    k2 = 2 * PEER_TOPK
    lane_groups = [slice(l * ROUTE_LANES, (l + 1) * ROUTE_LANES) for l in range(t // ROUTE_LANES)]
    heads_ref[...] = jnp.zeros_like(heads_ref)

    def half_scores(hd, ls):
        r1 = pl.multiple_of(hd * 2 * N_KEYS, 2 * N_KEYS)
        return s_ref[pl.ds(r1, N_KEYS), ls], s_ref[pl.ds(r1 + N_KEYS, N_KEYS), ls]

    def sort_head(hd, carry):
        for ls in lane_groups:
            s1, s2 = half_scores(hd, ls)
            ab = _sorted_top16(s1) + _sorted_top16(s2)
            sub = lax.broadcasted_iota(jnp.int32, ab[0].shape, 0)
            for q in range(k2):
                lists_ref[hd, q, :, ls] = ab[q]
                heads_ref[q, :, ls] = jnp.where(sub == hd, ab[q], heads_ref[q, :, ls])
        return carry

    lax.fori_loop(0, PEER_HEADS, sort_head, 0)

    for ls in lane_groups:
        ab = [heads_ref[q, :, ls] for q in range(k2)]
        n_of_rank, inv_z = _pair_threshold(ab[:PEER_TOPK], ab[PEER_TOPK:])
        for p in range(PEER_TOPK):
            stats_ref[p, :, ls] = n_of_rank[p]
        stats_ref[PEER_TOPK, :, ls] = inv_z

    def tables_head(hd, carry):
        for ls in lane_groups:
            s1, s2 = half_scores(hd, ls)
            ab = [lists_ref[hd, q, :, ls] for q in range(k2)]
            stats = [_row_to_all_rows(stats_ref[p, :, ls], hd) for p in range(PEER_TOPK + 1)]
            n, e1, r2, e2 = _key_tables(s1, s2, ab[:PEER_TOPK], ab[PEER_TOPK:], stats[:PEER_TOPK], stats[PEER_TOPK])
            n_ref[hd, :, ls] = n
            e1_ref[hd, :, ls] = e1
            r2_ref[hd, :, ls] = r2.astype(BF16)
            e2_ref[hd, :, ls] = e2.astype(BF16)
        return carry

    lax.fori_loop(0, PEER_HEADS, tables_head, 0)


def _dense_kernel(h_ref, u_ref, vt_ref, n_ref, e1_ref, r2_ref, e2_ref, x_ref, mod_ref, nf_ref, o_ref,
                  acc_ref, a_ref, p_ref, *, final_norm):
    c = pl.program_id(1)
    tb = h_ref.shape[0]
    d = vt_ref.shape[0]
    sub = SUBLANES
    xr, zr = DENSE_X_ROWS, DENSE_Z_ROWS
    n_groups = u_ref.shape[0] // zr

    @pl.when(c == 0)
    def _():
        acc_ref[...] = jnp.zeros_like(acc_ref)

    h = h_ref[...]

    def x_group(g):
        for j in range(g * zr // xr, (g + 1) * zr // xr):
            js = slice(j * xr, (j + 1) * xr)
            a_ref[js, :] = _dot_nt(u_ref[js, :], h)

    def mask_group(g):
        for ii in range(g * zr // N_KEYS, (g + 1) * zr // N_KEYS):
            rs = slice(ii * N_KEYS, (ii + 1) * N_KEYS)
            for l in range(tb // MASK_LANES):
                ls = slice(l * MASK_LANES, (l + 1) * MASK_LANES)
                w = jnp.zeros((N_KEYS // sub, sub, MASK_LANES), BF16)
                for hd in range(PEER_HEADS):
                    nb = jnp.broadcast_to(n_ref[hd, ii:ii + 1, ls], (sub, MASK_LANES)).astype(BF16)
                    eb = jnp.broadcast_to(e1_ref[hd, ii:ii + 1, ls], (sub, MASK_LANES)).astype(BF16)
                    r2 = r2_ref[hd, :, ls].reshape(N_KEYS // sub, sub, MASK_LANES)
                    e2 = e2_ref[hd, :, ls].reshape(N_KEYS // sub, sub, MASK_LANES)
                    w = w + jnp.where(r2 < nb[None], e2 * eb[None], jnp.zeros_like(e2))
                p_ref[rs, ls] = _gelu_tanh(a_ref[rs, ls].astype(BF16)) * w.reshape(N_KEYS, MASK_LANES)

    def z_group(g):
        ks = slice(g * zr, (g + 1) * zr)
        for m in range(d // xr):
            ms = slice(m * xr, (m + 1) * xr)
            acc_ref[ms, :] += jnp.dot(vt_ref[ms, ks], p_ref[ks, :], preferred_element_type=F32)

    x_group(0)
    for g in range(n_groups):
        if g + 1 < n_groups:
            x_group(g + 1)
        mask_group(g)
        if g > 0:
            z_group(g - 1)
    z_group(n_groups - 1)

    @pl.when(c == pl.num_programs(1) - 1)
    def _():
        y = x_ref[...] + mod_ref[5:6, :] * acc_ref[...].T
        if final_norm:
            ms = jnp.mean(y * y, axis=-1, keepdims=True)
            y = y * lax.rsqrt(ms + EPS) * nf_ref[...]
        o_ref[...] = y


def _rope_tables(seq_len, n_identity):
    axis_dim = HEAD_DIM // 2
    n_rows = seq_len // GRID_W
    row = jnp.repeat(jnp.arange(n_rows, dtype=F32), GRID_W)
    col = jnp.tile(jnp.arange(GRID_W, dtype=F32), n_rows)
    inv = ROPE_THETA ** (-jnp.arange(0, axis_dim, 2, dtype=F32) / axis_dim)
    ang_r = row[:, None] * inv
    ang_c = col[:, None] * inv
    cos = jnp.concatenate([jnp.cos(ang_r)] * 2 + [jnp.cos(ang_c)] * 2, axis=-1)
    sin = jnp.concatenate([-jnp.sin(ang_r), jnp.sin(ang_r), -jnp.sin(ang_c), jnp.sin(ang_c)], axis=-1)
    cos = jnp.concatenate([jnp.ones((n_identity, HEAD_DIM), F32), cos], axis=0)
    sin = jnp.concatenate([jnp.zeros((n_identity, HEAD_DIM), F32), sin], axis=0)
    return jnp.tile(cos, (1, N_HEADS)), jnp.tile(sin, (1, N_HEADS))


def _head_selectors(n_heads):
    c = jnp.arange(n_heads * HEAD_DIM)[:, None] // HEAD_DIM
    sel = (c == jnp.arange(128)[None, :]).astype(BF16)
    return sel, sel.T


def _peer_route(x, mods, norm_ffn3, wf_all, layer, cond_row):
    tt, d = x.shape
    tb = TOKEN_BLOCK
    n_keys2 = PEER_HEADS * 2 * N_KEYS
    tab = lambda dtype: jax.ShapeDtypeStruct((PEER_HEADS, N_KEYS, tt), dtype)
    tab_spec = pl.BlockSpec((PEER_HEADS, N_KEYS, tb), lambda b: (0, 0, b))
    row_spec = pl.BlockSpec((tb, d), lambda b: (b, 0))
    h, *tabs = pl.pallas_call(
        _route_kernel,
        grid=(tt // tb,),
        in_specs=[row_spec,
                  pl.BlockSpec((None, None, N_MOD, d), lambda b: (layer, cond_row(b, tb), 0, 0)),
                  pl.BlockSpec((None, 1, d), lambda b: (layer, 0, 0)),
                  pl.BlockSpec((None, n_keys2, d), lambda b: (layer, 0, 0))],
        out_specs=[row_spec, tab_spec, tab_spec, tab_spec, tab_spec],
        out_shape=[jax.ShapeDtypeStruct((tt, d), BF16), tab(F32), tab(F32), tab(BF16), tab(BF16)],
        scratch_shapes=[pltpu.VMEM((n_keys2, tb), F32),
                        pltpu.VMEM((PEER_HEADS, 2 * PEER_TOPK, SUBLANES, tb), F32),
                        pltpu.VMEM((2 * PEER_TOPK, SUBLANES, tb), F32),
                        pltpu.VMEM((PEER_TOPK + 1, SUBLANES, tb), F32)],
        compiler_params=_cparams("parallel"),
        name="peer_route",
    )(x, mods, norm_ffn3, wf_all)
    return h, tabs


def _peer_dense(h, u_bf, vt_bf, tabs, x, mods, layer, cond_row, norm_final, final_norm):
    tt, d = x.shape
    n_exp = u_bf.shape[1]
    dt, de = DENSE_TOKENS, DENSE_CHUNK
    n_tab, e1_tab, r2_tab, e2_tab = tabs
    dtab = pl.BlockSpec((PEER_HEADS, N_KEYS, dt), lambda b, e: (0, 0, b))
    rtab = pl.BlockSpec((PEER_HEADS, de // N_KEYS, dt), lambda b, e: (0, e, b))
    return pl.pallas_call(
        functools.partial(_dense_kernel, final_norm=final_norm),
        grid=(tt // dt, n_exp // de),
        in_specs=[
            pl.BlockSpec((dt, d), lambda b, e: (b, 0)),
            pl.BlockSpec((None, de, d), lambda b, e: (layer, e, 0)),
            pl.BlockSpec((None, d, de), lambda b, e: (layer, 0, e)),
            rtab, rtab, dtab, dtab,
            pl.BlockSpec((dt, d), lambda b, e: (b, 0)),
            pl.BlockSpec((None, None, N_MOD, d), lambda b, e: (layer, cond_row(b, dt), 0, 0)),
            pl.BlockSpec((1, d), lambda b, e: (0, 0)),
        ],
        out_specs=pl.BlockSpec((dt, d), lambda b, e: (b, 0)),
        out_shape=jax.ShapeDtypeStruct((tt, d), F32),
        scratch_shapes=[pltpu.VMEM((d, dt), F32), pltpu.VMEM((de, dt), F32), pltpu.VMEM((de, dt), BF16)],
        compiler_params=_cparams("parallel", "arbitrary"),
        name="peer_dense",
    )(h, u_bf, vt_bf, n_tab, e1_tab, r2_tab, e2_tab, x, mods, norm_final.reshape(1, d))


def kernel(x_prompt, x_sample, cache_k, cache_v, c, c_ctx, w_ada, b_ada, norm_mix, norm_ffn, norm_final,
           w_qkv, q_gain, k_gain, w_o, w_bcx, conv_w, conv_b, w_conv_out, w_pq, sub_keys, u_exp, v_exp):
    batch, seq, d = x_prompt.shape
    dec_batch, dec_seq, _ = x_sample.shape
    depth = w_ada.shape[0]
    past = cache_k.shape[2]
    tp = batch * seq
    ts = dec_batch * dec_seq
    tt = tp + ts
    tb = TOKEN_BLOCK
    assert seq % tb == 0 and dec_seq % tb == 0 and dec_batch + 1 <= N_COND_ROWS
    assert tp % DENSE_TOKENS == 0 and dec_seq % DENSE_TOKENS == 0
    assert d == N_HEADS * HEAD_DIM and seq == tb and past == tb and tp % dec_seq == 0
    npb = tp // tb
    bpp = seq // tb
    bps = dec_seq // tb
    nblk = tt // tb
    dq = N_HEADS * HEAD_DIM
    dk = N_KV_HEADS * HEAD_DIM
    n_exp = u_exp.shape[1]
    assert n_exp == N_KEYS * N_KEYS and n_exp % DENSE_CHUNK == 0

    def cond_row(b, rows=tb):
        t0 = b * rows
        return jnp.where(t0 < tp, 0, 1 + (t0 - tp) // dec_seq)

    def mod_spec(layer):
        return pl.BlockSpec((None, None, N_MOD, d), lambda b: (layer, cond_row(b), 0, 0))

    def row_spec(width, dtype_rows=tb):
        return pl.BlockSpec((dtype_rows, width), lambda b: (b, 0))

    def layer_vec(layer, width):
        return pl.BlockSpec((None, 1, width), lambda b: (layer, 0, 0))

    def const2(shape):
        return pl.BlockSpec(shape, lambda b: (0, 0))

    x = jnp.concatenate([x_prompt.reshape(tp, d), x_sample.reshape(ts, d)], axis=0)
    cond = jnp.zeros((N_COND_ROWS, d), F32).at[0].set(c_ctx).at[1:1 + dec_batch].set(c)
    mods = _modulation(cond, w_ada, b_ada).reshape(depth, N_COND_ROWS, N_MOD, d)

    wf_all = _fold_keys(w_pq, sub_keys)
    u_bf = u_exp.astype(BF16)
    vt_bf = jnp.transpose(v_exp, (0, 2, 1)).astype(BF16)
    w_qkv_bf = w_qkv.astype(BF16)
    w_o_bf = w_o.astype(BF16)
    w_bcx_bf = w_bcx.astype(BF16)
    w_co_bf = w_conv_out.astype(BF16)
    norm_mix3 = norm_mix.reshape(depth, 1, d)
    norm_ffn3 = norm_ffn.reshape(depth, 1, d)

    cos_t, sin_t = _rope_tables(dec_seq, tb)
    selq, selq_t = _head_selectors(N_HEADS)
    selk, selk_t = _head_selectors(N_KV_HEADS)
    scale = HEAD_DIM ** -0.5 * math.log2(math.e)
    qg_all = jnp.tile(q_gain, (1, N_HEADS)).reshape(-1, 1, dq) * scale
    kg_all = jnp.tile(k_gain, (1, N_KV_HEADS)).reshape(-1, 1, dk)
    ck_all = jnp.transpose(cache_k, (1, 0, 3, 2, 4)).astype(BF16)
    cv_all = jnp.transpose(cache_v, (1, 0, 3, 2, 4)).astype(BF16)

    def rope_block(b):
        return jnp.where(b < npb, 0, 1 + (b - npb) % bps)

    new_k, new_v = [], []
    for i in range(depth):
        j = i // 2
        if i % 2 == 0:
            q, k_hm, v_hm, kf, vf = pl.pallas_call(
                _qkv_kernel,
                grid=(nblk,),
                in_specs=[
                    row_spec(d), mod_spec(i), layer_vec(i, d),
                    pl.BlockSpec((None, d, dq + 2 * dk), lambda b, j=j: (j, 0, 0)),
                    pl.BlockSpec((None, 1, dq), lambda b, j=j: (j, 0, 0)),
                    pl.BlockSpec((None, 1, dk), lambda b, j=j: (j, 0, 0)),
                    const2((dq, 128)), const2((128, dq)), const2((dk, 128)), const2((128, dk)),
                    pl.BlockSpec((tb, dq), lambda b: (rope_block(b), 0)),
                    pl.BlockSpec((tb, dq), lambda b: (rope_block(b), 0)),
                ],
                out_specs=[
                    row_spec(dq),
                    pl.BlockSpec((N_KV_HEADS, tb, HEAD_DIM), lambda b: (0, b, 0)),
                    pl.BlockSpec((N_KV_HEADS, tb, HEAD_DIM), lambda b: (0, b, 0)),
                    row_spec(dk), row_spec(dk),
                ],
                out_shape=[
                    jax.ShapeDtypeStruct((tt, dq), BF16),
                    jax.ShapeDtypeStruct((N_KV_HEADS, tt, HEAD_DIM), BF16),
                    jax.ShapeDtypeStruct((N_KV_HEADS, tt, HEAD_DIM), BF16),
                    jax.ShapeDtypeStruct((tt, dk), F32),
                    jax.ShapeDtypeStruct((tt, dk), F32),
                ],
                compiler_params=_cparams("parallel"),
                name="qkv_project",
            )(x, mods, norm_mix3, w_qkv_bf, qg_all, kg_all, selq, selq_t, selk, selk_t, cos_t, sin_t)
            new_k.append(kf[:tp].reshape(batch, seq, N_KV_HEADS, HEAD_DIM))
            new_v.append(vf[:tp].reshape(batch, seq, N_KV_HEADS, HEAD_DIM))

            def sample_seq(u):
                return jnp.maximum(u - npb, 0) // bps

            kv_a = pl.BlockSpec((None, tb, HEAD_DIM), lambda u, g: (g, jnp.minimum(u, npb - 1), 0))
            kv_b = pl.BlockSpec((None, dec_seq, HEAD_DIM), lambda u, g: (g, tp // dec_seq + sample_seq(u), 0))
            kv_c = pl.BlockSpec((None, None, None, past, HEAD_DIM), lambda u, g, j=j: (j, sample_seq(u), g, 0, 0))
            attn = pl.pallas_call(
                functools.partial(_attn_kernel, n_prompt_blocks=npb),
                grid=(nblk, N_KV_HEADS),
                in_specs=[pl.BlockSpec((tb, dk), lambda u, g: (u, g)), kv_a, kv_a, kv_b, kv_b, kv_c, kv_c],
                out_specs=pl.BlockSpec((tb, dk), lambda u, g: (u, g)),
                out_shape=jax.ShapeDtypeStruct((tt, dq), BF16),
                compiler_params=_cparams("parallel", "parallel"),
                name="attention",
            )(q, k_hm, v_hm, k_hm, v_hm, ck_all, cv_all)

            x = pl.pallas_call(
                _proj_res_kernel,
                grid=(nblk,),
                in_specs=[row_spec(dq), pl.BlockSpec((None, dq, d), lambda b, j=j: (j, 0, 0)), row_spec(d), mod_spec(i)],
                out_specs=row_spec(d),
                out_shape=jax.ShapeDtypeStruct((tt, d), F32),
                compiler_params=_cparams("parallel"),
                name="attn_out_project",
            )(attn, w_o_bf, x, mods)
        else:
            u, gb = pl.pallas_call(
                _conv_in_kernel,
                grid=(nblk,),
                in_specs=[row_spec(d), mod_spec(i), layer_vec(i, d),
                          pl.BlockSpec((None, d, 3 * d), lambda b, j=j: (j, 0, 0))],
                out_specs=[row_spec(d), row_spec(d)],
                out_shape=[jax.ShapeDtypeStruct((tt, d), F32), jax.ShapeDtypeStruct((tt, d), BF16)],
                compiler_params=_cparams("parallel"),
                name="conv_in_project",
            )(x, mods, norm_mix3, w_bcx_bf)
            halo = tb // 8
            x = pl.pallas_call(
                functools.partial(_conv_out_kernel, n_prompt_blocks=npb, blocks_per_prompt_seq=bpp,
                                  blocks_per_sample_seq=bps),
                grid=(nblk,),
                in_specs=[
                    row_spec(d),
                    pl.BlockSpec((8, d), lambda b: (jnp.maximum(b * halo - 1, 0), 0)),
                    pl.BlockSpec((8, d), lambda b: (jnp.minimum((b + 1) * halo, nblk * halo - 1), 0)),
                    row_spec(d),
                    pl.BlockSpec((None, 3, d), lambda b, j=j: (j, 0, 0)),
                    pl.BlockSpec((None, 1, d), lambda b, j=j: (j, 0, 0)),
                    pl.BlockSpec((None, d, d), lambda b, j=j: (j, 0, 0)),
                    row_spec(d), mod_spec(i),
                ],
                out_specs=row_spec(d),
                out_shape=jax.ShapeDtypeStruct((tt, d), F32),
                compiler_params=_cparams("parallel"),
                name="conv_out_project",
            )(u, u, u, gb, conv_w, conv_b.reshape(-1, 1, d), w_co_bf, x, mods)

        h, tabs = _peer_route(x, mods, norm_ffn3, wf_all, i, cond_row)
        x = _peer_dense(h, u_bf, vt_bf, tabs, x, mods, i, cond_row, norm_final, final_norm=(i == depth - 1))

    y = x
    y_prompt = y[:tp].reshape(batch, seq, d)
    y_sample = y[tp:].reshape(dec_batch, dec_seq, d)
    return (y_prompt, y_sample, jnp.stack(new_k, axis=1), jnp.stack(new_v, axis=1))
```

```python
import functools
import math

import jax
import jax.numpy as jnp
from jax import lax
from jax.experimental import pallas as pl
from jax.experimental.pallas import tpu as pltpu

F32 = jnp.float32
BF16 = jnp.bfloat16

GRID_W = 64
N_HEADS = 16
N_KV_HEADS = 4
HEAD_DIM = 64
ROT_HALF = 16
ROPE_THETA = 10000.0
PEER_HEADS = 8
N_KEYS = 128
PEER_TOPK = 16
N_MOD = 6
EPS = 1e-6
N_COND_ROWS = 8
SUBLANES = 8

TOKEN_BLOCK = 256
DENSE_TOKENS = 512
DENSE_CHUNK = 2048
DENSE_X_ROWS = 256
DENSE_Z_ROWS = 512
MASK_LANES = 256
ROUTE_LANES = 128
VMEM_LIMIT = 52 * 1024 * 1024


def _cparams(*sem):
    return pltpu.CompilerParams(dimension_semantics=sem, vmem_limit_bytes=VMEM_LIMIT)


def _split_bf16(a):
    hi = a.astype(BF16)
    lo = (a - hi.astype(F32)).astype(BF16)
    return hi, lo


def _norm_mod(x, nw, shift, scale):
    ms = jnp.mean(x * x, axis=-1, keepdims=True)
    return (x * lax.rsqrt(ms + EPS)) * nw * (1.0 + scale) + shift


def _gelu_tanh(x):
    c0 = math.sqrt(2.0 / math.pi)
    inner = x * (c0 + (c0 * 0.044715) * (x * x))
    return (0.5 * x) * (1.0 + jnp.tanh(inner))


def _dot_nt(a, b):
    return lax.dot_general(a, b, (((1,), (1,)), ((), ())), preferred_element_type=F32)


def _mod_kernel(c_ref, w_ref, b_ref, o_ref):
    c = c_ref[...]
    s_hi, s_lo = _split_bf16(jax.nn.silu(c))
    w_hi, w_lo = _split_bf16(w_ref[...])
    acc = jnp.dot(s_hi, w_hi, preferred_element_type=F32)
    acc += jnp.dot(s_hi, w_lo, preferred_element_type=F32)
    acc += jnp.dot(s_lo, w_hi, preferred_element_type=F32)
    o_ref[...] = acc + b_ref[...]


def _modulation(cond, w_ada, b_ada):
    depth, d, nd = w_ada.shape
    nb = 1536
    return pl.pallas_call(
        _mod_kernel,
        grid=(depth, nd // nb),
        in_specs=[
            pl.BlockSpec((N_COND_ROWS, d), lambda l, n: (0, 0)),
            pl.BlockSpec((None, d, nb), lambda l, n: (l, 0, n)),
            pl.BlockSpec((None, 1, nb), lambda l, n: (l, 0, n)),
        ],
        out_specs=pl.BlockSpec((None, N_COND_ROWS, nb), lambda l, n: (l, 0, n)),
        out_shape=jax.ShapeDtypeStruct((depth, N_COND_ROWS, nd), F32),
        compiler_params=_cparams("parallel", "parallel"),
        name="adaln_modulation",
    )(cond, w_ada, b_ada.reshape(depth, 1, nd))


def _fold_kernel(sk_ref, w_ref, o_ref):
    k_hi, k_lo = _split_bf16(sk_ref[...])
    w_hi, w_lo = _split_bf16(w_ref[...])
    acc = _dot_nt(k_hi, w_hi) + _dot_nt(k_hi, w_lo) + _dot_nt(k_lo, w_hi)
    o_ref[...] = acc.astype(BF16)


def _fold_keys(w_pq, sub_keys):
    depth, d, _ = w_pq.shape
    dkh = sub_keys.shape[-1]
    n_parts = PEER_HEADS * 2
    return pl.pallas_call(
        _fold_kernel,
        grid=(depth, n_parts),
        in_specs=[
            pl.BlockSpec((None, None, None, N_KEYS, dkh), lambda l, p: (l, p // 2, p % 2, 0, 0)),
            pl.BlockSpec((None, d, dkh), lambda l, p: (l, 0, p)),
        ],
        out_specs=pl.BlockSpec((None, N_KEYS, d), lambda l, p: (l, p, 0)),
        out_shape=jax.ShapeDtypeStruct((depth, n_parts * N_KEYS, d), BF16),
        compiler_params=_cparams("parallel", "parallel"),
        name="peer_fold_keys",
    )(sub_keys, w_pq)


def _head_rms_scale(q, sel, sel_t):
    hi, lo = _split_bf16(q * q)
    ss = jnp.dot(hi, sel, preferred_element_type=F32) + jnp.dot(lo, sel, preferred_element_type=F32)
    r = lax.rsqrt(ss * (1.0 / HEAD_DIM) + EPS)
    r_hi, r_lo = _split_bf16(r)
    return jnp.dot(r_hi, sel_t, preferred_element_type=F32) + jnp.dot(r_lo, sel_t, preferred_element_type=F32)


def _rope(x, cos, sin_signed):
    n = x.shape[1]
    lane = lax.broadcasted_iota(jnp.int32, x.shape, 1)
    upper = (lane & ROT_HALF) != 0
    partner = jnp.where(upper, pltpu.roll(x, ROT_HALF, axis=1), pltpu.roll(x, n - ROT_HALF, axis=1))
    return x * cos + partner * sin_signed


def _qkv_kernel(x_ref, mod_ref, nw_ref, w_ref, qg_ref, kg_ref, selq_ref, selqt_ref, selk_ref, selkt_ref,
                cos_ref, sin_ref, q_ref, k_ref, v_ref, kf_ref, vf_ref):
    dq = N_HEADS * HEAD_DIM
    dk = N_KV_HEADS * HEAD_DIM
    h = _norm_mod(x_ref[...], nw_ref[...], mod_ref[0:1, :], mod_ref[1:2, :]).astype(BF16)
    qkv = jnp.dot(h, w_ref[...], preferred_element_type=F32)
    q = qkv[:, :dq]
    k = qkv[:, dq:dq + dk]
    v = qkv[:, dq + dk:]
    cos = cos_ref[...]
    sin = sin_ref[...]
    qn = q * _head_rms_scale(q, selq_ref[...], selqt_ref[...]) * qg_ref[...]
    q_ref[...] = _rope(qn, cos, sin).astype(BF16)
    kn = k * _head_rms_scale(k, selk_ref[...], selkt_ref[...]) * kg_ref[...]
    kf_ref[...] = kn
    vf_ref[...] = v
    kr = _rope(kn, cos[:, :dk], sin[:, :dk]).astype(BF16)
    vb = v.astype(BF16)
    for g in range(N_KV_HEADS):
        k_ref[g] = kr[:, g * HEAD_DIM:(g + 1) * HEAD_DIM]
        v_ref[g] = vb[:, g * HEAD_DIM:(g + 1) * HEAD_DIM]


def _softmax_pv(qh, parts):
    scores = [_dot_nt(qh, k) for k, _ in parts]
    m = scores[0].max(axis=-1, keepdims=True)
    for s in scores[1:]:
        m = jnp.maximum(m, s.max(axis=-1, keepdims=True))
    num = None
    den = None
    for s, (_, v) in zip(scores, parts):
        p = jnp.exp2(s - m)
        d = p.sum(axis=-1, keepdims=True)
        o = jnp.dot(p.astype(BF16), v, preferred_element_type=F32)
        num = o if num is None else num + o
        den = d if den is None else den + d
    return num / den


def _attn_kernel(q_ref, ka_ref, va_ref, kb_ref, vb_ref, kc_ref, vc_ref, o_ref, s_ref, p_ref, l_ref, *,
                 n_prompt_blocks):
    u = pl.program_id(0)
    group = N_HEADS // N_KV_HEADS
    head = lambda hh: slice(hh * HEAD_DIM, (hh + 1) * HEAD_DIM)

    @pl.when(u < n_prompt_blocks)
    def _():
        parts = [(ka_ref[...], va_ref[...])]
        for hh in range(group):
            o_ref[:, head(hh)] = _softmax_pv(q_ref[:, head(hh)], parts).astype(BF16)

    @pl.when(u >= n_prompt_blocks)
    def _():
        lb = kb_ref.shape[0]

        def scores(hh):
            qh = q_ref[:, head(hh)]
            s_ref[hh, :, 0:lb] = _dot_nt(qh, kb_ref[...])
            s_ref[hh, :, lb:] = _dot_nt(qh, kc_ref[...])

        def softmax(hh):
            s = s_ref[hh]
            p = jnp.exp2(s - s.max(axis=-1, keepdims=True))
            l_ref[hh] = jnp.broadcast_to(p.sum(axis=-1, keepdims=True), l_ref.shape[1:])
            p_ref[hh] = p.astype(BF16)

        def weighted_values(hh):
            o = jnp.dot(p_ref[hh, :, 0:lb], vb_ref[...], preferred_element_type=F32)
            o = o + jnp.dot(p_ref[hh, :, lb:], vc_ref[...], preferred_element_type=F32)
            o_ref[:, head(hh)] = (o / l_ref[hh, :, 0:1]).astype(BF16)

        scores(0)
        for hh in range(group):
            if hh + 1 < group:
                scores(hh + 1)
            softmax(hh)
            if hh > 0:
                weighted_values(hh - 1)
        weighted_values(group - 1)


def _proj_res_kernel(a_ref, w_ref, x_ref, mod_ref, o_ref):
    o_ref[...] = x_ref[...] + mod_ref[2:3, :] * jnp.dot(a_ref[...], w_ref[...], preferred_element_type=F32)


def _conv_in_kernel(x_ref, mod_ref, nw_ref, w_ref, u_ref, gb_ref):
    d = x_ref.shape[1]
    h = _norm_mod(x_ref[...], nw_ref[...], mod_ref[0:1, :], mod_ref[1:2, :]).astype(BF16)
    bcx = jnp.dot(h, w_ref[...], preferred_element_type=F32)
    gb_ref[...] = bcx[:, :d].astype(BF16)
    u_ref[...] = bcx[:, d:2 * d] * bcx[:, 2 * d:]


def _conv_out_kernel(u_ref, up_ref, un_ref, gb_ref, cw_ref, cb_ref, w_ref, x_ref, mod_ref, o_ref, *,
                     n_prompt_blocks, blocks_per_prompt_seq, blocks_per_sample_seq):
    tb = u_ref.shape[0]
    b = pl.program_id(0)
    bs = b - n_prompt_blocks
    pos = jnp.where(b < n_prompt_blocks, b % blocks_per_prompt_seq, bs % blocks_per_sample_seq)
    per_seq = jnp.where(b < n_prompt_blocks, blocks_per_prompt_seq, blocks_per_sample_seq)
    u = u_ref[...]
    prev_row = jnp.where(pos == 0, 0.0, up_ref[7:8, :])
    next_row = jnp.where(pos == per_seq - 1, 0.0, un_ref[0:1, :])
    row = lax.broadcasted_iota(jnp.int32, u.shape, 0)
    u_m1 = jnp.where(row == 0, prev_row, pltpu.roll(u, 1, axis=0))
    u_p1 = jnp.where(row == tb - 1, next_row, pltpu.roll(u, tb - 1, axis=0))
    conv = u_m1 * cw_ref[0:1, :] + u * cw_ref[1:2, :] + u_p1 * cw_ref[2:3, :] + cb_ref[...]
    y = (gb_ref[...].astype(F32) * conv).astype(BF16)
    o_ref[...] = x_ref[...] + mod_ref[2:3, :] * jnp.dot(y, w_ref[...], preferred_element_type=F32)


def _oddeven_merge(lo, hi, r):
    step = r * 2
    if step < hi - lo:
        yield from _oddeven_merge(lo, hi, step)
        yield from _oddeven_merge(lo + r, hi, step)
        yield from [(i, i + r) for i in range(lo + r, hi - r, step)]
    else:
        yield (lo, lo + r)


def _oddeven_sort(lo, hi):
    if hi - lo >= 1:
        mid = lo + (hi - lo) // 2
        yield from _oddeven_sort(lo, mid)
        yield from _oddeven_sort(mid + 1, hi)
        yield from _oddeven_merge(lo, hi, 1)


def _bitonic_merge_pairs(n):
    out, s = [], n // 2
    while s >= 1:
        out += [(i, i + s) for i in range(n) if (i // s) % 2 == 0]
        s //= 2
    return out


SORT16 = tuple(_oddeven_sort(0, 15))
SORT8 = tuple(_oddeven_sort(0, 7))
BITONIC16 = tuple(_bitonic_merge_pairs(16))


def _apply_net(net, v):
    v = list(v)
    for i, j in net:
        hi, lo = jnp.maximum(v[i], v[j]), jnp.minimum(v[i], v[j])
        v[i], v[j] = hi, lo
    return v


def _merge_top16(a, b):
    return _apply_net(BITONIC16, [jnp.maximum(a[i], b[PEER_TOPK - 1 - i]) for i in range(PEER_TOPK)])


def _sorted_top16(x):
    v = _apply_net(SORT16, [x[SUBLANES * r:SUBLANES * (r + 1)] for r in range(N_KEYS // SUBLANES)])
    for shift in (4, 2, 1):
        v = _merge_top16(v, [pltpu.roll(u, shift, axis=0) for u in v])
    return v


def _pair_threshold(a, b):
    shape = a[0].shape
    neg = jnp.full(shape, -jnp.inf, F32)
    c = {(p, q): a[p - 1] + b[q - 1] for p in range(1, PEER_TOPK + 1) for q in range(1, PEER_TOPK // p + 1)}
    row1 = [c[(1, q)] for q in range(1, 17)]
    col1 = [c[(p, 1)] for p in range(2, 17)] + [neg]
    row2 = [c[(2, q)] for q in range(2, 9)] + [neg]
    col2 = [c[(p, 2)] for p in range(3, 9)] + [neg, neg]
    rest = [c[(3, 3)], c[(3, 4)], c[(3, 5)], c[(4, 3)], c[(5, 3)], c[(4, 4)], neg, neg]
    m2 = _apply_net(BITONIC16, row2 + col2[::-1])
    m3 = _apply_net(SORT8, rest) + [neg] * 8
    top = _merge_top16(_merge_top16(row1, col1), _merge_top16(m2, m3))
    tau = top[PEER_TOPK - 1]
    z = jnp.zeros(shape, F32)
    for v in top:
        z = z + jnp.exp(v - top[0])
    n_of_rank = []
    for p in range(1, PEER_TOPK + 1):
        cnt = jnp.zeros(shape, F32)
        for q in range(1, PEER_TOPK // p + 1):
            cnt = cnt + jnp.where(c[(p, q)] >= tau, 1.0, 0.0)
        n_of_rank.append(cnt)
    return n_of_rank, 1.0 / z


def _key_tables(s1, s2, a, b, n_of_rank, inv_z):
    t = s1.shape[1]
    n_rows, e1_rows, r2_rows, e2_rows = [], [], [], []
    for r in range(N_KEYS // SUBLANES):
        x1 = s1[SUBLANES * r:SUBLANES * (r + 1)]
        x2 = s2[SUBLANES * r:SUBLANES * (r + 1)]
        n = jnp.zeros((SUBLANES, t), F32)
        r2 = jnp.full((SUBLANES, t), float(PEER_TOPK), F32)
        for q in range(PEER_TOPK - 1, -1, -1):
            n = jnp.where(x1 >= a[q], n_of_rank[q], n)
            r2 = jnp.where(x2 >= b[q], float(q), r2)
        n_rows.append(n)
        r2_rows.append(r2)
        e1_rows.append(jnp.exp(x1 - a[0]) * inv_z)
        e2_rows.append(jnp.exp(x2 - b[0]))
    cat = lambda rows: jnp.concatenate(rows, axis=0)
    return cat(n_rows), cat(e1_rows), cat(r2_rows), cat(e2_rows)


def _row_to_all_rows(x, row):
    sub = lax.broadcasted_iota(jnp.int32, x.shape, 0)
    y = jnp.where(sub == row, x, 0.0)
    for shift in (4, 2, 1):
        y = y + pltpu.roll(y, shift, axis=0)
    return y


def _route_kernel(x_ref, mod_ref, nw_ref, wf_ref, h_ref, n_ref, e1_ref, r2_ref, e2_ref,
                  s_ref, lists_ref, heads_ref, stats_ref):
    h = _norm_mod(x_ref[...], nw_ref[...], mod_ref[3:4, :], mod_ref[4:5, :]).astype(BF16)
    h_ref[...] = h
    s_ref[...] = _dot_nt(wf_ref[...], h)
    t = s_ref.shape[1]
    k2 = 2 * PEER_TOPK
    lane_groups = [slice(l * ROUTE_LANES, (l + 1) * ROUTE_LANES) for l in range(t // ROUTE_LANES)]
    heads_ref[...] = jnp.zeros_like(heads_ref)

    def half_scores(hd, ls):
        r1 = pl.multiple_of(hd * 2 * N_KEYS, 2 * N_KEYS)
        return s_ref[pl.ds(r1, N_KEYS), ls], s_ref[pl.ds(r1 + N_KEYS, N_KEYS), ls]

    def sort_head(hd, carry):
        for ls in lane_groups:
            s1, s2 = half_scores(hd, ls)
            ab = _sorted_top16(s1) + _sorted_top16(s2)
            sub = lax.broadcasted_iota(jnp.int32, ab[0].shape, 0)
            for q in range(k2):
                lists_ref[hd, q, :, ls] = ab[q]
                heads_ref[q, :, ls] = jnp.where(sub == hd, ab[q], heads_ref[q, :, ls])
        return carry

    lax.fori_loop(0, PEER_HEADS, sort_head, 0)

    for ls in lane_groups:
        ab = [heads_ref[q, :, ls] for q in range(k2)]
        n_of_rank, inv_z = _pair_threshold(ab[:PEER_TOPK], ab[PEER_TOPK:])
        for p in range(PEER_TOPK):
            stats_ref[p, :, ls] = n_of_rank[p]
        stats_ref[PEER_TOPK, :, ls] = inv_z

    def tables_head(hd, carry):
        for ls in lane_groups:
            s1, s2 = half_scores(hd, ls)
            ab = [lists_ref[hd, q, :, ls] for q in range(k2)]
            stats = [_row_to_all_rows(stats_ref[p, :, ls], hd) for p in range(PEER_TOPK + 1)]
            n, e1, r2, e2 = _key_tables(s1, s2, ab[:PEER_TOPK], ab[PEER_TOPK:], stats[:PEER_TOPK], stats[PEER_TOPK])
            n_ref[hd, :, ls] = n
            e1_ref[hd, :, ls] = e1
            r2_ref[hd, :, ls] = r2.astype(BF16)
            e2_ref[hd, :, ls] = e2.astype(BF16)
        return carry

    lax.fori_loop(0, PEER_HEADS, tables_head, 0)


def _dense_kernel(h_ref, u_ref, vt_ref, n_ref, e1_ref, r2_ref, e2_ref, x_ref, mod_ref, nf_ref, o_ref,
                  acc_ref, a_ref, p_ref, *, final_norm):
    c = pl.program_id(1)
    tb = h_ref.shape[0]
    d = vt_ref.shape[0]
    sub = SUBLANES
    xr, zr = DENSE_X_ROWS, DENSE_Z_ROWS
    n_groups = u_ref.shape[0] // zr

    @pl.when(c == 0)
    def _():
        acc_ref[...] = jnp.zeros_like(acc_ref)

    h = h_ref[...]

    def x_group(g):
        for j in range(g * zr // xr, (g + 1) * zr // xr):
            js = slice(j * xr, (j + 1) * xr)
            a_ref[js, :] = _dot_nt(u_ref[js, :], h)

    def mask_group(g):
        for ii in range(g * zr // N_KEYS, (g + 1) * zr // N_KEYS):
            rs = slice(ii * N_KEYS, (ii + 1) * N_KEYS)
            for l in range(tb // MASK_LANES):
                ls = slice(l * MASK_LANES, (l + 1) * MASK_LANES)
                w = jnp.zeros((N_KEYS // sub, sub, MASK_LANES), BF16)
                for hd in range(PEER_HEADS):
                    nb = jnp.broadcast_to(n_ref[hd, ii:ii + 1, ls], (sub, MASK_LANES)).astype(BF16)
                    eb = jnp.broadcast_to(e1_ref[hd, ii:ii + 1, ls], (sub, MASK_LANES)).astype(BF16)
                    r2 = r2_ref[hd, :, ls].reshape(N_KEYS // sub, sub, MASK_LANES)
                    e2 = e2_ref[hd, :, ls].reshape(N_KEYS // sub, sub, MASK_LANES)
                    w = w + jnp.where(r2 < nb[None], e2 * eb[None], jnp.zeros_like(e2))
                p_ref[rs, ls] = _gelu_tanh(a_ref[rs, ls].astype(BF16)) * w.reshape(N_KEYS, MASK_LANES)

    def z_group(g):
        ks = slice(g * zr, (g + 1) * zr)
        for m in range(d // xr):
            ms = slice(m * xr, (m + 1) * xr)
            acc_ref[ms, :] += jnp.dot(vt_ref[ms, ks], p_ref[ks, :], preferred_element_type=F32)

    x_group(0)
    for g in range(n_groups):
        if g + 1 < n_groups:
            x_group(g + 1)
        mask_group(g)
        if g > 0:
            z_group(g - 1)
    z_group(n_groups - 1)

    @pl.when(c == pl.num_programs(1) - 1)
    def _():
        y = x_ref[...] + mod_ref[5:6, :] * acc_ref[...].T
        if final_norm:
            ms = jnp.mean(y * y, axis=-1, keepdims=True)
            y = y * lax.rsqrt(ms + EPS) * nf_ref[...]
        o_ref[...] = y


def _rope_tables(seq_len, n_identity):
    axis_dim = HEAD_DIM // 2
    n_rows = seq_len // GRID_W
    row = jnp.repeat(jnp.arange(n_rows, dtype=F32), GRID_W)
    col = jnp.tile(jnp.arange(GRID_W, dtype=F32), n_rows)
    inv = ROPE_THETA ** (-jnp.arange(0, axis_dim, 2, dtype=F32) / axis_dim)
    ang_r = row[:, None] * inv
    ang_c = col[:, None] * inv
    cos = jnp.concatenate([jnp.cos(ang_r)] * 2 + [jnp.cos(ang_c)] * 2, axis=-1)
    sin = jnp.concatenate([-jnp.sin(ang_r), jnp.sin(ang_r), -jnp.sin(ang_c), jnp.sin(ang_c)], axis=-1)
    cos = jnp.concatenate([jnp.ones((n_identity, HEAD_DIM), F32), cos], axis=0)
    sin = jnp.concatenate([jnp.zeros((n_identity, HEAD_DIM), F32), sin], axis=0)
    return jnp.tile(cos, (1, N_HEADS)), jnp.tile(sin, (1, N_HEADS))


def _head_selectors(n_heads):
    c = jnp.arange(n_heads * HEAD_DIM)[:, None] // HEAD_DIM
    sel = (c == jnp.arange(128)[None, :]).astype(BF16)
    return sel, sel.T


def _attention(q, k_hm, v_hm, ck_all, cv_all, layer_j, n_prompt_blocks, blocks_per_sample_seq, dec_seq, past):
    tt, dq = q.shape
    tb = TOKEN_BLOCK
    dk = N_KV_HEADS * HEAD_DIM
    npb, bps = n_prompt_blocks, blocks_per_sample_seq
    tp = npb * tb
    group = N_HEADS // N_KV_HEADS

    def sample_seq(u):
        return jnp.maximum(u - npb, 0) // bps

    kv_a = pl.BlockSpec((None, tb, HEAD_DIM), lambda u, g: (g, jnp.minimum(u, npb - 1), 0))
    kv_b = pl.BlockSpec((None, dec_seq, HEAD_DIM), lambda u, g: (g, tp // dec_seq + sample_seq(u), 0))
    kv_c = pl.BlockSpec((None, None, None, past, HEAD_DIM), lambda u, g: (layer_j, sample_seq(u), g, 0, 0))
    return pl.pallas_call(
        functools.partial(_attn_kernel, n_prompt_blocks=npb),
        grid=(tt // tb, N_KV_HEADS),
        in_specs=[pl.BlockSpec((tb, dk), lambda u, g: (u, g)), kv_a, kv_a, kv_b, kv_b, kv_c, kv_c],
        out_specs=pl.BlockSpec((tb, dk), lambda u, g: (u, g)),
        out_shape=jax.ShapeDtypeStruct((tt, dq), BF16),
        scratch_shapes=[pltpu.VMEM((group, tb, dec_seq + past), F32), pltpu.VMEM((group, tb, dec_seq + past), BF16),
                        pltpu.VMEM((group, tb, 128), F32)],
        compiler_params=_cparams("parallel", "parallel"),
        name="attention",
    )(q, k_hm, v_hm, k_hm, v_hm, ck_all, cv_all)


def _peer_route(x, mods, norm_ffn3, wf_all, layer, cond_row):
    tt, d = x.shape
    tb = TOKEN_BLOCK
    n_keys2 = PEER_HEADS * 2 * N_KEYS
    tab = lambda dtype: jax.ShapeDtypeStruct((PEER_HEADS, N_KEYS, tt), dtype)
    tab_spec = pl.BlockSpec((PEER_HEADS, N_KEYS, tb), lambda b: (0, 0, b))
    row_spec = pl.BlockSpec((tb, d), lambda b: (b, 0))
    h, *tabs = pl.pallas_call(
        _route_kernel,
        grid=(tt // tb,),
        in_specs=[row_spec,
                  pl.BlockSpec((None, None, N_MOD, d), lambda b: (layer, cond_row(b, tb), 0, 0)),
                  pl.BlockSpec((None, 1, d), lambda b: (layer, 0, 0)),
                  pl.BlockSpec((None, n_keys2, d), lambda b: (layer, 0, 0))],
        out_specs=[row_spec, tab_spec, tab_spec, tab_spec, tab_spec],
        out_shape=[jax.ShapeDtypeStruct((tt, d), BF16), tab(F32), tab(F32), tab(BF16), tab(BF16)],
        scratch_shapes=[pltpu.VMEM((n_keys2, tb), F32),
                        pltpu.VMEM((PEER_HEADS, 2 * PEER_TOPK, SUBLANES, tb), F32),
                        pltpu.VMEM((2 * PEER_TOPK, SUBLANES, tb), F32),
                        pltpu.VMEM((PEER_TOPK + 1, SUBLANES, tb), F32)],
        compiler_params=_cparams("parallel"),
        name="peer_route",
    )(x, mods, norm_ffn3, wf_all)
    return h, tabs


def _peer_dense(h, u_bf, vt_bf, tabs, x, mods, layer, cond_row, norm_final, final_norm):
    tt, d = x.shape
    n_exp = u_bf.shape[1]
    dt, de = DENSE_TOKENS, DENSE_CHUNK
    n_tab, e1_tab, r2_tab, e2_tab = tabs
    dtab = pl.BlockSpec((PEER_HEADS, N_KEYS, dt), lambda b, e: (0, 0, b))
    rtab = pl.BlockSpec((PEER_HEADS, de // N_KEYS, dt), lambda b, e: (0, e, b))
    return pl.pallas_call(
        functools.partial(_dense_kernel, final_norm=final_norm),
        grid=(tt // dt, n_exp // de),
        in_specs=[
            pl.BlockSpec((dt, d), lambda b, e: (b, 0)),
            pl.BlockSpec((None, de, d), lambda b, e: (layer, e, 0)),
            pl.BlockSpec((None, d, de), lambda b, e: (layer, 0, e)),
            rtab, rtab, dtab, dtab,
            pl.BlockSpec((dt, d), lambda b, e: (b, 0)),
            pl.BlockSpec((None, None, N_MOD, d), lambda b, e: (layer, cond_row(b, dt), 0, 0)),
            pl.BlockSpec((1, d), lambda b, e: (0, 0)),
        ],
        out_specs=pl.BlockSpec((dt, d), lambda b, e: (b, 0)),
        out_shape=jax.ShapeDtypeStruct((tt, d), F32),
        scratch_shapes=[pltpu.VMEM((d, dt), F32), pltpu.VMEM((de, dt), F32), pltpu.VMEM((de, dt), BF16)],
        compiler_params=_cparams("parallel", "arbitrary"),
        name="peer_dense",
    )(h, u_bf, vt_bf, n_tab, e1_tab, r2_tab, e2_tab, x, mods, norm_final.reshape(1, d))


def kernel(x_prompt, x_sample, cache_k, cache_v, c, c_ctx, w_ada, b_ada, norm_mix, norm_ffn, norm_final,
           w_qkv, q_gain, k_gain, w_o, w_bcx, conv_w, conv_b, w_conv_out, w_pq, sub_keys, u_exp, v_exp):
    batch, seq, d = x_prompt.shape
    dec_batch, dec_seq, _ = x_sample.shape
    depth = w_ada.shape[0]
    past = cache_k.shape[2]
    tp = batch * seq
    ts = dec_batch * dec_seq
    tt = tp + ts
    tb = TOKEN_BLOCK
    assert seq % tb == 0 and dec_seq % tb == 0 and dec_batch + 1 <= N_COND_ROWS
    assert tp % DENSE_TOKENS == 0 and dec_seq % DENSE_TOKENS == 0
    assert d == N_HEADS * HEAD_DIM and seq == tb and past == tb and tp % dec_seq == 0
    npb = tp // tb
    bpp = seq // tb
    bps = dec_seq // tb
    nblk = tt // tb
    dq = N_HEADS * HEAD_DIM
    dk = N_KV_HEADS * HEAD_DIM
    n_exp = u_exp.shape[1]
    assert n_exp == N_KEYS * N_KEYS and n_exp % DENSE_CHUNK == 0

    def cond_row(b, rows=tb):
        t0 = b * rows
        return jnp.where(t0 < tp, 0, 1 + (t0 - tp) // dec_seq)

    def mod_spec(layer):
        return pl.BlockSpec((None, None, N_MOD, d), lambda b: (layer, cond_row(b), 0, 0))

    def row_spec(width, dtype_rows=tb):
        return pl.BlockSpec((dtype_rows, width), lambda b: (b, 0))

    def layer_vec(layer, width):
        return pl.BlockSpec((None, 1, width), lambda b: (layer, 0, 0))

    def const2(shape):
        return pl.BlockSpec(shape, lambda b: (0, 0))

    x = jnp.concatenate([x_prompt.reshape(tp, d), x_sample.reshape(ts, d)], axis=0)
    cond = jnp.zeros((N_COND_ROWS, d), F32).at[0].set(c_ctx).at[1:1 + dec_batch].set(c)
    mods = _modulation(cond, w_ada, b_ada).reshape(depth, N_COND_ROWS, N_MOD, d)

    wf_all = _fold_keys(w_pq, sub_keys)
    u_bf = u_exp.astype(BF16)
    vt_bf = jnp.transpose(v_exp, (0, 2, 1)).astype(BF16)
    w_qkv_bf = w_qkv.astype(BF16)
    w_o_bf = w_o.astype(BF16)
    w_bcx_bf = w_bcx.astype(BF16)
    w_co_bf = w_conv_out.astype(BF16)
    norm_mix3 = norm_mix.reshape(depth, 1, d)
    norm_ffn3 = norm_ffn.reshape(depth, 1, d)

    cos_t, sin_t = _rope_tables(dec_seq, tb)
    selq, selq_t = _head_selectors(N_HEADS)
    selk, selk_t = _head_selectors(N_KV_HEADS)
    scale = HEAD_DIM ** -0.5 * math.log2(math.e)
    qg_all = jnp.tile(q_gain, (1, N_HEADS)).reshape(-1, 1, dq) * scale
    kg_all = jnp.tile(k_gain, (1, N_KV_HEADS)).reshape(-1, 1, dk)
    ck_all = jnp.transpose(cache_k, (1, 0, 3, 2, 4)).astype(BF16)
    cv_all = jnp.transpose(cache_v, (1, 0, 3, 2, 4)).astype(BF16)

    def rope_block(b):
        return jnp.where(b < npb, 0, 1 + (b - npb) % bps)

    new_k, new_v = [], []
    for i in range(depth):
        j = i // 2
        if i % 2 == 0:
            q, k_hm, v_hm, kf, vf = pl.pallas_call(
                _qkv_kernel,
                grid=(nblk,),
                in_specs=[
                    row_spec(d), mod_spec(i), layer_vec(i, d),
                    pl.BlockSpec((None, d, dq + 2 * dk), lambda b, j=j: (j, 0, 0)),
                    pl.BlockSpec((None, 1, dq), lambda b, j=j: (j, 0, 0)),
                    pl.BlockSpec((None, 1, dk), lambda b, j=j: (j, 0, 0)),
                    const2((dq, 128)), const2((128, dq)), const2((dk, 128)), const2((128, dk)),
                    pl.BlockSpec((tb, dq), lambda b: (rope_block(b), 0)),
                    pl.BlockSpec((tb, dq), lambda b: (rope_block(b), 0)),
                ],
                out_specs=[
                    row_spec(dq),
                    pl.BlockSpec((N_KV_HEADS, tb, HEAD_DIM), lambda b: (0, b, 0)),
                    pl.BlockSpec((N_KV_HEADS, tb, HEAD_DIM), lambda b: (0, b, 0)),
                    row_spec(dk), row_spec(dk),
                ],
                out_shape=[
                    jax.ShapeDtypeStruct((tt, dq), BF16),
                    jax.ShapeDtypeStruct((N_KV_HEADS, tt, HEAD_DIM), BF16),
                    jax.ShapeDtypeStruct((N_KV_HEADS, tt, HEAD_DIM), BF16),
                    jax.ShapeDtypeStruct((tt, dk), F32),
                    jax.ShapeDtypeStruct((tt, dk), F32),
                ],
                compiler_params=_cparams("parallel"),
                name="qkv_project",
            )(x, mods, norm_mix3, w_qkv_bf, qg_all, kg_all, selq, selq_t, selk, selk_t, cos_t, sin_t)
            new_k.append(kf[:tp].reshape(batch, seq, N_KV_HEADS, HEAD_DIM))
            new_v.append(vf[:tp].reshape(batch, seq, N_KV_HEADS, HEAD_DIM))

            attn = _attention(q, k_hm, v_hm, ck_all, cv_all, j, npb, bps, dec_seq, past)

            x = pl.pallas_call(
                _proj_res_kernel,
                grid=(nblk,),
                in_specs=[row_spec(dq), pl.BlockSpec((None, dq, d), lambda b, j=j: (j, 0, 0)), row_spec(d), mod_spec(i)],
                out_specs=row_spec(d),
                out_shape=jax.ShapeDtypeStruct((tt, d), F32),
                compiler_params=_cparams("parallel"),
                name="attn_out_project",
            )(attn, w_o_bf, x, mods)
        else:
            u, gb = pl.pallas_call(
                _conv_in_kernel,
                grid=(nblk,),
                in_specs=[row_spec(d), mod_spec(i), layer_vec(i, d),
                          pl.BlockSpec((None, d, 3 * d), lambda b, j=j: (j, 0, 0))],
                out_specs=[row_spec(d), row_spec(d)],
                out_shape=[jax.ShapeDtypeStruct((tt, d), F32), jax.ShapeDtypeStruct((tt, d), BF16)],
                compiler_params=_cparams("parallel"),
                name="conv_in_project",
            )(x, mods, norm_mix3, w_bcx_bf)
            halo = tb // 8
            x = pl.pallas_call(
                functools.partial(_conv_out_kernel, n_prompt_blocks=npb, blocks_per_prompt_seq=bpp,
                                  blocks_per_sample_seq=bps),
                grid=(nblk,),
                in_specs=[
                    row_spec(d),
                    pl.BlockSpec((8, d), lambda b: (jnp.maximum(b * halo - 1, 0), 0)),
                    pl.BlockSpec((8, d), lambda b: (jnp.minimum((b + 1) * halo, nblk * halo - 1), 0)),
                    row_spec(d),
                    pl.BlockSpec((None, 3, d), lambda b, j=j: (j, 0, 0)),
                    pl.BlockSpec((None, 1, d), lambda b, j=j: (j, 0, 0)),
                    pl.BlockSpec((None, d, d), lambda b, j=j: (j, 0, 0)),
                    row_spec(d), mod_spec(i),
                ],
                out_specs=row_spec(d),
                out_shape=jax.ShapeDtypeStruct((tt, d), F32),
                compiler_params=_cparams("parallel"),
                name="conv_out_project",
            )(u, u, u, gb, conv_w, conv_b.reshape(-1, 1, d), w_co_bf, x, mods)

        h, tabs = _peer_route(x, mods, norm_ffn3, wf_all, i, cond_row)
        x = _peer_dense(h, u_bf, vt_bf, tabs, x, mods, i, cond_row, norm_final, final_norm=(i == depth - 1))

    y = x
    y_prompt = y[:tp].reshape(batch, seq, d)
    y_sample = y[tp:].reshape(dec_batch, dec_seq, d)
    return (y_prompt, y_sample, jnp.stack(new_k, axis=1), jnp.stack(new_v, axis=1))
```

```python
import functools
import math

import jax
import jax.numpy as jnp
from jax import lax
from jax.experimental import pallas as pl
from jax.experimental.pallas import tpu as pltpu

F32 = jnp.float32
BF16 = jnp.bfloat16

GRID_W = 64
N_HEADS = 16
N_KV_HEADS = 4
HEAD_DIM = 64
ROT_HALF = 16
ROPE_THETA = 10000.0
PEER_HEADS = 8
N_KEYS = 128
PEER_TOPK = 16
N_MOD = 6
EPS = 1e-6
N_COND_ROWS = 8
SUBLANES = 8

TOKEN_BLOCK = 256
DENSE_TOKENS = 512
DENSE_CHUNK = 2048
DENSE_X_ROWS = 256
DENSE_GROUP_BOUNDS = (0, 768, 1536, 1792, 2048)
DENSE_X_LEAD = 2
MASK_LANES = 256
ROUTE_LANES = 128
VMEM_LIMIT = 52 * 1024 * 1024


def _cparams(*sem):
    return pltpu.CompilerParams(dimension_semantics=sem, vmem_limit_bytes=VMEM_LIMIT)


def _split_bf16(a):
    hi = a.astype(BF16)
    lo = (a - hi.astype(F32)).astype(BF16)
    return hi, lo


def _norm_mod(x, nw, shift, scale):
    ms = jnp.mean(x * x, axis=-1, keepdims=True)
    return (x * lax.rsqrt(ms + EPS)) * nw * (1.0 + scale) + shift


def _gelu_tanh(x):
    c0 = math.sqrt(2.0 / math.pi)
    inner = x * (c0 + (c0 * 0.044715) * (x * x))
    return (0.5 * x) * (1.0 + jnp.tanh(inner))


def _dot_nt(a, b):
    return lax.dot_general(a, b, (((1,), (1,)), ((), ())), preferred_element_type=F32)


def _mod_kernel(c_ref, w_ref, b_ref, o_ref):
    c = c_ref[...]
    s_hi, s_lo = _split_bf16(jax.nn.silu(c))
    w_hi, w_lo = _split_bf16(w_ref[...])
    acc = jnp.dot(s_hi, w_hi, preferred_element_type=F32)
    acc += jnp.dot(s_hi, w_lo, preferred_element_type=F32)
    acc += jnp.dot(s_lo, w_hi, preferred_element_type=F32)
    o_ref[...] = acc + b_ref[...]


def _modulation(cond, w_ada, b_ada):
    depth, d, nd = w_ada.shape
    nb = 1536
    return pl.pallas_call(
        _mod_kernel,
        grid=(depth, nd // nb),
        in_specs=[
            pl.BlockSpec((N_COND_ROWS, d), lambda l, n: (0, 0)),
            pl.BlockSpec((None, d, nb), lambda l, n: (l, 0, n)),
            pl.BlockSpec((None, 1, nb), lambda l, n: (l, 0, n)),
        ],
        out_specs=pl.BlockSpec((None, N_COND_ROWS, nb), lambda l, n: (l, 0, n)),
        out_shape=jax.ShapeDtypeStruct((depth, N_COND_ROWS, nd), F32),
        compiler_params=_cparams("parallel", "parallel"),
        name="adaln_modulation",
    )(cond, w_ada, b_ada.reshape(depth, 1, nd))


def _fold_kernel(sk_ref, w_ref, o_ref):
    k_hi, k_lo = _split_bf16(sk_ref[...])
    w_hi, w_lo = _split_bf16(w_ref[...])
    acc = _dot_nt(k_hi, w_hi) + _dot_nt(k_hi, w_lo) + _dot_nt(k_lo, w_hi)
    o_ref[...] = acc.astype(BF16)


def _fold_keys(w_pq, sub_keys):
    depth, d, _ = w_pq.shape
    dkh = sub_keys.shape[-1]
    n_parts = PEER_HEADS * 2
    return pl.pallas_call(
        _fold_kernel,
        grid=(depth, n_parts),
        in_specs=[
            pl.BlockSpec((None, None, None, N_KEYS, dkh), lambda l, p: (l, p // 2, p % 2, 0, 0)),
            pl.BlockSpec((None, d, dkh), lambda l, p: (l, 0, p)),
        ],
        out_specs=pl.BlockSpec((None, N_KEYS, d), lambda l, p: (l, p, 0)),
        out_shape=jax.ShapeDtypeStruct((depth, n_parts * N_KEYS, d), BF16),
        compiler_params=_cparams("parallel", "parallel"),
        name="peer_fold_keys",
    )(sub_keys, w_pq)


def _head_rms_scale(q, sel, sel_t):
    hi, lo = _split_bf16(q * q)
    ss = jnp.dot(hi, sel, preferred_element_type=F32) + jnp.dot(lo, sel, preferred_element_type=F32)
    r = lax.rsqrt(ss * (1.0 / HEAD_DIM) + EPS)
    r_hi, r_lo = _split_bf16(r)
    return jnp.dot(r_hi, sel_t, preferred_element_type=F32) + jnp.dot(r_lo, sel_t, preferred_element_type=F32)


def _rope(x, cos, sin_signed):
    n = x.shape[1]
    lane = lax.broadcasted_iota(jnp.int32, x.shape, 1)
    upper = (lane & ROT_HALF) != 0
    partner = jnp.where(upper, pltpu.roll(x, ROT_HALF, axis=1), pltpu.roll(x, n - ROT_HALF, axis=1))
    return x * cos + partner * sin_signed


def _qkv_kernel(x_ref, mod_ref, nw_ref, w_ref, qg_ref, kg_ref, selq_ref, selqt_ref, selk_ref, selkt_ref,
                cos_ref, sin_ref, q_ref, k_ref, v_ref, kf_ref, vf_ref):
    dq = N_HEADS * HEAD_DIM
    dk = N_KV_HEADS * HEAD_DIM
    h = _norm_mod(x_ref[...], nw_ref[...], mod_ref[0:1, :], mod_ref[1:2, :]).astype(BF16)
    qkv = jnp.dot(h, w_ref[...], preferred_element_type=F32)
    q = qkv[:, :dq]
    k = qkv[:, dq:dq + dk]
    v = qkv[:, dq + dk:]
    cos = cos_ref[...]
    sin = sin_ref[...]
    qn = q * _head_rms_scale(q, selq_ref[...], selqt_ref[...]) * qg_ref[...]
    q_ref[...] = _rope(qn, cos, sin).astype(BF16)
    kn = k * _head_rms_scale(k, selk_ref[...], selkt_ref[...]) * kg_ref[...]
    kf_ref[...] = kn
    vf_ref[...] = v
    kr = _rope(kn, cos[:, :dk], sin[:, :dk]).astype(BF16)
    vb = v.astype(BF16)
    for g in range(N_KV_HEADS):
        k_ref[g] = kr[:, g * HEAD_DIM:(g + 1) * HEAD_DIM]
        v_ref[g] = vb[:, g * HEAD_DIM:(g + 1) * HEAD_DIM]


def _softmax_pv(qh, parts):
    scores = [_dot_nt(qh, k) for k, _ in parts]
    m = scores[0].max(axis=-1, keepdims=True)
    for s in scores[1:]:
        m = jnp.maximum(m, s.max(axis=-1, keepdims=True))
    num = None
    den = None
    for s, (_, v) in zip(scores, parts):
        p = jnp.exp2(s - m)
        d = p.sum(axis=-1, keepdims=True)
        o = jnp.dot(p.astype(BF16), v, preferred_element_type=F32)
        num = o if num is None else num + o
        den = d if den is None else den + d
    return num / den


def _attn_kernel(q_ref, ka_ref, va_ref, kb_ref, vb_ref, kc_ref, vc_ref, o_ref, s_ref, p_ref, l_ref, *,
                 n_prompt_blocks):
    u = pl.program_id(0)
    group = N_HEADS // N_KV_HEADS
    head = lambda hh: slice(hh * HEAD_DIM, (hh + 1) * HEAD_DIM)

    @pl.when(u < n_prompt_blocks)
    def _():
        parts = [(ka_ref[...], va_ref[...])]
        for hh in range(group):
            o_ref[:, head(hh)] = _softmax_pv(q_ref[:, head(hh)], parts).astype(BF16)

    @pl.when(u >= n_prompt_blocks)
    def _():
        lb = kb_ref.shape[0]

        def scores(hh):
            qh = q_ref[:, head(hh)]
            s_ref[hh, :, 0:lb] = _dot_nt(qh, kb_ref[...])
            s_ref[hh, :, lb:] = _dot_nt(qh, kc_ref[...])

        def softmax(hh):
            s = s_ref[hh]
            p = jnp.exp2(s - s.max(axis=-1, keepdims=True))
            l_ref[hh] = jnp.broadcast_to(p.sum(axis=-1, keepdims=True), l_ref.shape[1:])
            p_ref[hh] = p.astype(BF16)

        def weighted_values(hh):
            o = jnp.dot(p_ref[hh, :, 0:lb], vb_ref[...], preferred_element_type=F32)
            o = o + jnp.dot(p_ref[hh, :, lb:], vc_ref[...], preferred_element_type=F32)
            o_ref[:, head(hh)] = (o / l_ref[hh, :, 0:1]).astype(BF16)

        scores(0)
        for hh in range(group):
            if hh + 1 < group:
                scores(hh + 1)
            softmax(hh)
            if hh > 0:
                weighted_values(hh - 1)
        weighted_values(group - 1)


def _proj_res_kernel(a_ref, w_ref, x_ref, mod_ref, o_ref):
    o_ref[...] = x_ref[...] + mod_ref[2:3, :] * jnp.dot(a_ref[...], w_ref[...], preferred_element_type=F32)


def _conv_in_kernel(x_ref, mod_ref, nw_ref, w_ref, u_ref, gb_ref):
    d = x_ref.shape[1]
    h = _norm_mod(x_ref[...], nw_ref[...], mod_ref[0:1, :], mod_ref[1:2, :]).astype(BF16)
    bcx = jnp.dot(h, w_ref[...], preferred_element_type=F32)
    gb_ref[...] = bcx[:, :d].astype(BF16)
    u_ref[...] = bcx[:, d:2 * d] * bcx[:, 2 * d:]


def _conv_out_kernel(u_ref, up_ref, un_ref, gb_ref, cw_ref, cb_ref, w_ref, x_ref, mod_ref, o_ref, *,
                     n_prompt_blocks, blocks_per_prompt_seq, blocks_per_sample_seq):
    tb = u_ref.shape[0]
    b = pl.program_id(0)
    bs = b - n_prompt_blocks
    pos = jnp.where(b < n_prompt_blocks, b % blocks_per_prompt_seq, bs % blocks_per_sample_seq)
    per_seq = jnp.where(b < n_prompt_blocks, blocks_per_prompt_seq, blocks_per_sample_seq)
    u = u_ref[...]
    prev_row = jnp.where(pos == 0, 0.0, up_ref[7:8, :])
    next_row = jnp.where(pos == per_seq - 1, 0.0, un_ref[0:1, :])
    row = lax.broadcasted_iota(jnp.int32, u.shape, 0)
    u_m1 = jnp.where(row == 0, prev_row, pltpu.roll(u, 1, axis=0))
    u_p1 = jnp.where(row == tb - 1, next_row, pltpu.roll(u, tb - 1, axis=0))
    conv = u_m1 * cw_ref[0:1, :] + u * cw_ref[1:2, :] + u_p1 * cw_ref[2:3, :] + cb_ref[...]
    y = (gb_ref[...].astype(F32) * conv).astype(BF16)
    o_ref[...] = x_ref[...] + mod_ref[2:3, :] * jnp.dot(y, w_ref[...], preferred_element_type=F32)


def _oddeven_merge(lo, hi, r):
    step = r * 2
    if step < hi - lo:
        yield from _oddeven_merge(lo, hi, step)
        yield from _oddeven_merge(lo + r, hi, step)
        yield from [(i, i + r) for i in range(lo + r, hi - r, step)]
    else:
        yield (lo, lo + r)


def _oddeven_sort(lo, hi):
    if hi - lo >= 1:
        mid = lo + (hi - lo) // 2
        yield from _oddeven_sort(lo, mid)
        yield from _oddeven_sort(mid + 1, hi)
        yield from _oddeven_merge(lo, hi, 1)


def _bitonic_merge_pairs(n):
    out, s = [], n // 2
    while s >= 1:
        out += [(i, i + s) for i in range(n) if (i // s) % 2 == 0]
        s //= 2
    return out


SORT16 = tuple(_oddeven_sort(0, 15))
SORT8 = tuple(_oddeven_sort(0, 7))
BITONIC16 = tuple(_bitonic_merge_pairs(16))


def _apply_net(net, v):
    v = list(v)
    for i, j in net:
        hi, lo = jnp.maximum(v[i], v[j]), jnp.minimum(v[i], v[j])
        v[i], v[j] = hi, lo
    return v


def _merge_top16(a, b):
    return _apply_net(BITONIC16, [jnp.maximum(a[i], b[PEER_TOPK - 1 - i]) for i in range(PEER_TOPK)])


def _sorted_top16(x):
    v = _apply_net(SORT16, [x[SUBLANES * r:SUBLANES * (r + 1)] for r in range(N_KEYS // SUBLANES)])
    for shift in (4, 2, 1):
        v = _merge_top16(v, [pltpu.roll(u, shift, axis=0) for u in v])
    return v


def _pair_threshold(a, b):
    shape = a[0].shape
    neg = jnp.full(shape, -jnp.inf, F32)
    c = {(p, q): a[p - 1] + b[q - 1] for p in range(1, PEER_TOPK + 1) for q in range(1, PEER_TOPK // p + 1)}
    row1 = [c[(1, q)] for q in range(1, 17)]
    col1 = [c[(p, 1)] for p in range(2, 17)] + [neg]
    row2 = [c[(2, q)] for q in range(2, 9)] + [neg]
    col2 = [c[(p, 2)] for p in range(3, 9)] + [neg, neg]
    rest = [c[(3, 3)], c[(3, 4)], c[(3, 5)], c[(4, 3)], c[(5, 3)], c[(4, 4)], neg, neg]
    m2 = _apply_net(BITONIC16, row2 + col2[::-1])
    m3 = _apply_net(SORT8, rest) + [neg] * 8
    top = _merge_top16(_merge_top16(row1, col1), _merge_top16(m2, m3))
    tau = top[PEER_TOPK - 1]
    z = jnp.zeros(shape, F32)
    for v in top:
        z = z + jnp.exp(v - top[0])
    n_of_rank = []
    for p in range(1, PEER_TOPK + 1):
        cnt = jnp.zeros(shape, F32)
        for q in range(1, PEER_TOPK // p + 1):
            cnt = cnt + jnp.where(c[(p, q)] >= tau, 1.0, 0.0)
        n_of_rank.append(cnt)
    return n_of_rank, 1.0 / z


def _key_tables(s1, s2, a, b, n_of_rank, inv_z):
    t = s1.shape[1]
    n_rows, e1_rows, r2_rows, e2_rows = [], [], [], []
    for r in range(N_KEYS // SUBLANES):
        x1 = s1[SUBLANES * r:SUBLANES * (r + 1)]
        x2 = s2[SUBLANES * r:SUBLANES * (r + 1)]
        n = jnp.zeros((SUBLANES, t), F32)
        r2 = jnp.full((SUBLANES, t), float(PEER_TOPK), F32)
        for q in range(PEER_TOPK - 1, -1, -1):
            n = jnp.where(x1 >= a[q], n_of_rank[q], n)
            r2 = jnp.where(x2 >= b[q], float(q), r2)
        n_rows.append(n)
        r2_rows.append(r2)
        e1_rows.append(jnp.exp(x1 - a[0]) * inv_z)
        e2_rows.append(jnp.exp(x2 - b[0]))
    cat = lambda rows: jnp.concatenate(rows, axis=0)
    return cat(n_rows), cat(e1_rows), cat(r2_rows), cat(e2_rows)


def _row_to_all_rows(x, row):
    sub = lax.broadcasted_iota(jnp.int32, x.shape, 0)
    y = jnp.where(sub == row, x, 0.0)
    for shift in (4, 2, 1):
        y = y + pltpu.roll(y, shift, axis=0)
    return y


def _route_kernel(x_ref, mod_ref, nw_ref, wf_ref, h_ref, n_ref, e1_ref, r2_ref, e2_ref,
                  s_ref, lists_ref, heads_ref, stats_ref):
    h = _norm_mod(x_ref[...], nw_ref[...], mod_ref[3:4, :], mod_ref[4:5, :]).astype(BF16)
    h_ref[...] = h
    s_ref[...] = _dot_nt(wf_ref[...], h)
    t = s_ref.shape[1]
    k2 = 2 * PEER_TOPK
    lane_groups = [slice(l * ROUTE_LANES, (l + 1) * ROUTE_LANES) for l in range(t // ROUTE_LANES)]
    heads_ref[...] = jnp.zeros_like(heads_ref)

    def half_scores(hd, ls):
        r1 = pl.multiple_of(hd * 2 * N_KEYS, 2 * N_KEYS)
        return s_ref[pl.ds(r1, N_KEYS), ls], s_ref[pl.ds(r1 + N_KEYS, N_KEYS), ls]

    def sort_head(hd, carry):
        for ls in lane_groups:
            s1, s2 = half_scores(hd, ls)
            ab = _sorted_top16(s1) + _sorted_top16(s2)
            sub = lax.broadcasted_iota(jnp.int32, ab[0].shape, 0)
            for q in range(k2):
                lists_ref[hd, q, :, ls] = ab[q]
                heads_ref[q, :, ls] = jnp.where(sub == hd, ab[q], heads_ref[q, :, ls])
        return carry

    lax.fori_loop(0, PEER_HEADS, sort_head, 0)

    for ls in lane_groups:
        ab = [heads_ref[q, :, ls] for q in range(k2)]
        n_of_rank, inv_z = _pair_threshold(ab[:PEER_TOPK], ab[PEER_TOPK:])
        for p in range(PEER_TOPK):
            stats_ref[p, :, ls] = n_of_rank[p]
        stats_ref[PEER_TOPK, :, ls] = inv_z

    def tables_head(hd, carry):
        for ls in lane_groups:
            s1, s2 = half_scores(hd, ls)
            ab = [lists_ref[hd, q, :, ls] for q in range(k2)]
            stats = [_row_to_all_rows(stats_ref[p, :, ls], hd) for p in range(PEER_TOPK + 1)]
            n, e1, r2, e2 = _key_tables(s1, s2, ab[:PEER_TOPK], ab[PEER_TOPK:], stats[:PEER_TOPK], stats[PEER_TOPK])
            n_ref[hd, :, ls] = n
            e1_ref[hd, :, ls] = e1
            r2_ref[hd, :, ls] = r2.astype(BF16)
            e2_ref[hd, :, ls] = e2.astype(BF16)
        return carry

    lax.fori_loop(0, PEER_HEADS, tables_head, 0)


def _dense_kernel(h_ref, u_ref, vt_ref, n_ref, e1_ref, r2_ref, e2_ref, x_ref, mod_ref, nf_ref, o_ref,
                  acc_ref, a_ref, p_ref, *, final_norm):
    c = pl.program_id(1)
    tb = h_ref.shape[0]
    d = vt_ref.shape[0]
    sub = SUBLANES
    xr = DENSE_X_ROWS
    bounds = DENSE_GROUP_BOUNDS
    n_groups = len(bounds) - 1

    @pl.when(c == 0)
    def _():
        acc_ref[...] = jnp.zeros_like(acc_ref)

    h = h_ref[...]

    def x_group(g):
        for r0 in range(bounds[g], bounds[g + 1], xr):
            js = slice(r0, r0 + xr)
            a_ref[js, :] = _dot_nt(u_ref[js, :], h)

    def mask_group(g):
        for ii in range(bounds[g] // N_KEYS, bounds[g + 1] // N_KEYS):
            rs = slice(ii * N_KEYS, (ii + 1) * N_KEYS)
            for l in range(tb // MASK_LANES):
                ls = slice(l * MASK_LANES, (l + 1) * MASK_LANES)
                w = jnp.zeros((N_KEYS // sub, sub, MASK_LANES), BF16)
                for hd in range(PEER_HEADS):
                    nb = jnp.broadcast_to(n_ref[hd, ii:ii + 1, ls], (sub, MASK_LANES)).astype(BF16)
                    eb = jnp.broadcast_to(e1_ref[hd, ii:ii + 1, ls], (sub, MASK_LANES)).astype(BF16)
                    r2 = r2_ref[hd, :, ls].reshape(N_KEYS // sub, sub, MASK_LANES)
                    e2 = e2_ref[hd, :, ls].reshape(N_KEYS // sub, sub, MASK_LANES)
                    w = w + jnp.where(r2 < nb[None], e2 * eb[None], jnp.zeros_like(e2))
                p_ref[rs, ls] = _gelu_tanh(a_ref[rs, ls].astype(BF16)) * w.reshape(N_KEYS, MASK_LANES)

    def z_group(g):
        ks = slice(bounds[g], bounds[g + 1])
        for m in range(d // xr):
            ms = slice(m * xr, (m + 1) * xr)
            acc_ref[ms, :] += jnp.dot(vt_ref[ms, ks], p_ref[ks, :], preferred_element_type=F32)

    for g in range(min(DENSE_X_LEAD, n_groups)):
        x_group(g)
    for g in range(n_groups):
        if g + DENSE_X_LEAD < n_groups:
            x_group(g + DENSE_X_LEAD)
        mask_group(g)
        if g > 0:
            z_group(g - 1)
    z_group(n_groups - 1)

    @pl.when(c == pl.num_programs(1) - 1)
    def _():
        y = x_ref[...] + mod_ref[5:6, :] * acc_ref[...].T
        if final_norm:
            ms = jnp.mean(y * y, axis=-1, keepdims=True)
            y = y * lax.rsqrt(ms + EPS) * nf_ref[...]
        o_ref[...] = y


def _rope_tables(seq_len, n_identity):
    axis_dim = HEAD_DIM // 2
    n_rows = seq_len // GRID_W
    row = jnp.repeat(jnp.arange(n_rows, dtype=F32), GRID_W)
    col = jnp.tile(jnp.arange(GRID_W, dtype=F32), n_rows)
    inv = ROPE_THETA ** (-jnp.arange(0, axis_dim, 2, dtype=F32) / axis_dim)
    ang_r = row[:, None] * inv
    ang_c = col[:, None] * inv
    cos = jnp.concatenate([jnp.cos(ang_r)] * 2 + [jnp.cos(ang_c)] * 2, axis=-1)
    sin = jnp.concatenate([-jnp.sin(ang_r), jnp.sin(ang_r), -jnp.sin(ang_c), jnp.sin(ang_c)], axis=-1)
    cos = jnp.concatenate([jnp.ones((n_identity, HEAD_DIM), F32), cos], axis=0)
    sin = jnp.concatenate([jnp.zeros((n_identity, HEAD_DIM), F32), sin], axis=0)
    return jnp.tile(cos, (1, N_HEADS)), jnp.tile(sin, (1, N_HEADS))


def _head_selectors(n_heads):
    c = jnp.arange(n_heads * HEAD_DIM)[:, None] // HEAD_DIM
    sel = (c == jnp.arange(128)[None, :]).astype(BF16)
    return sel, sel.T


def _attention(q, k_hm, v_hm, ck_all, cv_all, layer_j, n_prompt_blocks, blocks_per_sample_seq, dec_seq, past):
    tt, dq = q.shape
    tb = TOKEN_BLOCK
    dk = N_KV_HEADS * HEAD_DIM
    npb, bps = n_prompt_blocks, blocks_per_sample_seq
    tp = npb * tb
    group = N_HEADS // N_KV_HEADS

    def sample_seq(u):
        return jnp.maximum(u - npb, 0) // bps

    kv_a = pl.BlockSpec((None, tb, HEAD_DIM), lambda u, g: (g, jnp.minimum(u, npb - 1), 0))
    kv_b = pl.BlockSpec((None, dec_seq, HEAD_DIM), lambda u, g: (g, tp // dec_seq + sample_seq(u), 0))
    kv_c = pl.BlockSpec((None, None, None, past, HEAD_DIM), lambda u, g: (layer_j, sample_seq(u), g, 0, 0))
    return pl.pallas_call(
        functools.partial(_attn_kernel, n_prompt_blocks=npb),
        grid=(tt // tb, N_KV_HEADS),
        in_specs=[pl.BlockSpec((tb, dk), lambda u, g: (u, g)), kv_a, kv_a, kv_b, kv_b, kv_c, kv_c],
        out_specs=pl.BlockSpec((tb, dk), lambda u, g: (u, g)),
        out_shape=jax.ShapeDtypeStruct((tt, dq), BF16),
        scratch_shapes=[pltpu.VMEM((group, tb, dec_seq + past), F32), pltpu.VMEM((group, tb, dec_seq + past), BF16),
                        pltpu.VMEM((group, tb, 128), F32)],
        compiler_params=_cparams("parallel", "parallel"),
        name="attention",
    )(q, k_hm, v_hm, k_hm, v_hm, ck_all, cv_all)


def _peer_route(x, mods, norm_ffn3, wf_all, layer, cond_row):
    tt, d = x.shape
    tb = TOKEN_BLOCK
    n_keys2 = PEER_HEADS * 2 * N_KEYS
    tab = lambda dtype: jax.ShapeDtypeStruct((PEER_HEADS, N_KEYS, tt), dtype)
    tab_spec = pl.BlockSpec((PEER_HEADS, N_KEYS, tb), lambda b: (0, 0, b))
    row_spec = pl.BlockSpec((tb, d), lambda b: (b, 0))
    h, *tabs = pl.pallas_call(
        _route_kernel,
        grid=(tt // tb,),
        in_specs=[row_spec,
                  pl.BlockSpec((None, None, N_MOD, d), lambda b: (layer, cond_row(b, tb), 0, 0)),
                  pl.BlockSpec((None, 1, d), lambda b: (layer, 0, 0)),
                  pl.BlockSpec((None, n_keys2, d), lambda b: (layer, 0, 0))],
        out_specs=[row_spec, tab_spec, tab_spec, tab_spec, tab_spec],
        out_shape=[jax.ShapeDtypeStruct((tt, d), BF16), tab(F32), tab(F32), tab(BF16), tab(BF16)],
        scratch_shapes=[pltpu.VMEM((n_keys2, tb), F32),
                        pltpu.VMEM((PEER_HEADS, 2 * PEER_TOPK, SUBLANES, tb), F32),
                        pltpu.VMEM((2 * PEER_TOPK, SUBLANES, tb), F32),
                        pltpu.VMEM((PEER_TOPK + 1, SUBLANES, tb), F32)],
        compiler_params=_cparams("parallel"),
        name="peer_route",
    )(x, mods, norm_ffn3, wf_all)
    return h, tabs


def _peer_dense(h, u_bf, vt_bf, tabs, x, mods, layer, cond_row, norm_final, final_norm):
    tt, d = x.shape
    n_exp = u_bf.shape[1]
    dt, de = DENSE_TOKENS, DENSE_CHUNK
    n_tab, e1_tab, r2_tab, e2_tab = tabs
    dtab = pl.BlockSpec((PEER_HEADS, N_KEYS, dt), lambda b, e: (0, 0, b))
    rtab = pl.BlockSpec((PEER_HEADS, de // N_KEYS, dt), lambda b, e: (0, e, b))
    return pl.pallas_call(
        functools.partial(_dense_kernel, final_norm=final_norm),
        grid=(tt // dt, n_exp // de),
        in_specs=[
            pl.BlockSpec((dt, d), lambda b, e: (b, 0)),
            pl.BlockSpec((None, de, d), lambda b, e: (layer, e, 0)),
            pl.BlockSpec((None, d, de), lambda b, e: (layer, 0, e)),
            rtab, rtab, dtab, dtab,
            pl.BlockSpec((dt, d), lambda b, e: (b, 0)),
            pl.BlockSpec((None, None, N_MOD, d), lambda b, e: (layer, cond_row(b, dt), 0, 0)),
            pl.BlockSpec((1, d), lambda b, e: (0, 0)),
        ],
        out_specs=pl.BlockSpec((dt, d), lambda b, e: (b, 0)),
        out_shape=jax.ShapeDtypeStruct((tt, d), F32),
        scratch_shapes=[pltpu.VMEM((d, dt), F32), pltpu.VMEM((de, dt), F32), pltpu.VMEM((de, dt), BF16)],
        compiler_params=_cparams("parallel", "arbitrary"),
        name="peer_dense",
    )(h, u_bf, vt_bf, n_tab, e1_tab, r2_tab, e2_tab, x, mods, norm_final.reshape(1, d))


def kernel(x_prompt, x_sample, cache_k, cache_v, c, c_ctx, w_ada, b_ada, norm_mix, norm_ffn, norm_final,
           w_qkv, q_gain, k_gain, w_o, w_bcx, conv_w, conv_b, w_conv_out, w_pq, sub_keys, u_exp, v_exp):
    batch, seq, d = x_prompt.shape
    dec_batch, dec_seq, _ = x_sample.shape
    depth = w_ada.shape[0]
    past = cache_k.shape[2]
    tp = batch * seq
    ts = dec_batch * dec_seq
    tt = tp + ts
    tb = TOKEN_BLOCK
    assert seq % tb == 0 and dec_seq % tb == 0 and dec_batch + 1 <= N_COND_ROWS
    assert tp % DENSE_TOKENS == 0 and dec_seq % DENSE_TOKENS == 0
    assert d == N_HEADS * HEAD_DIM and seq == tb and past == tb and tp % dec_seq == 0
    npb = tp // tb
    bpp = seq // tb
    bps = dec_seq // tb
    nblk = tt // tb
    dq = N_HEADS * HEAD_DIM
    dk = N_KV_HEADS * HEAD_DIM
    n_exp = u_exp.shape[1]
    assert n_exp == N_KEYS * N_KEYS and n_exp % DENSE_CHUNK == 0
    assert DENSE_GROUP_BOUNDS[0] == 0 and DENSE_GROUP_BOUNDS[-1] == DENSE_CHUNK
    assert all(b % DENSE_X_ROWS == 0 for b in DENSE_GROUP_BOUNDS) and d % DENSE_X_ROWS == 0

    def cond_row(b, rows=tb):
        t0 = b * rows
        return jnp.where(t0 < tp, 0, 1 + (t0 - tp) // dec_seq)

    def mod_spec(layer):
        return pl.BlockSpec((None, None, N_MOD, d), lambda b: (layer, cond_row(b), 0, 0))

    def row_spec(width, dtype_rows=tb):
        return pl.BlockSpec((dtype_rows, width), lambda b: (b, 0))

    def layer_vec(layer, width):
        return pl.BlockSpec((None, 1, width), lambda b: (layer, 0, 0))

    def const2(shape):
        return pl.BlockSpec(shape, lambda b: (0, 0))

    x = jnp.concatenate([x_prompt.reshape(tp, d), x_sample.reshape(ts, d)], axis=0)
    cond = jnp.zeros((N_COND_ROWS, d), F32).at[0].set(c_ctx).at[1:1 + dec_batch].set(c)
    mods = _modulation(cond, w_ada, b_ada).reshape(depth, N_COND_ROWS, N_MOD, d)

    wf_all = _fold_keys(w_pq, sub_keys)
    u_bf = u_exp.astype(BF16)
    vt_bf = jnp.transpose(v_exp, (0, 2, 1)).astype(BF16)
    w_qkv_bf = w_qkv.astype(BF16)
    w_o_bf = w_o.astype(BF16)
    w_bcx_bf = w_bcx.astype(BF16)
    w_co_bf = w_conv_out.astype(BF16)
    norm_mix3 = norm_mix.reshape(depth, 1, d)
    norm_ffn3 = norm_ffn.reshape(depth, 1, d)

    cos_t, sin_t = _rope_tables(dec_seq, tb)
    selq, selq_t = _head_selectors(N_HEADS)
    selk, selk_t = _head_selectors(N_KV_HEADS)
    scale = HEAD_DIM ** -0.5 * math.log2(math.e)
    qg_all = jnp.tile(q_gain, (1, N_HEADS)).reshape(-1, 1, dq) * scale
    kg_all = jnp.tile(k_gain, (1, N_KV_HEADS)).reshape(-1, 1, dk)
    ck_all = jnp.transpose(cache_k, (1, 0, 3, 2, 4)).astype(BF16)
    cv_all = jnp.transpose(cache_v, (1, 0, 3, 2, 4)).astype(BF16)

    def rope_block(b):
        return jnp.where(b < npb, 0, 1 + (b - npb) % bps)

    new_k, new_v = [], []
    for i in range(depth):
        j = i // 2
        if i % 2 == 0:
            q, k_hm, v_hm, kf, vf = pl.pallas_call(
                _qkv_kernel,
                grid=(nblk,),
                in_specs=[
                    row_spec(d), mod_spec(i), layer_vec(i, d),
                    pl.BlockSpec((None, d, dq + 2 * dk), lambda b, j=j: (j, 0, 0)),
                    pl.BlockSpec((None, 1, dq), lambda b, j=j: (j, 0, 0)),
                    pl.BlockSpec((None, 1, dk), lambda b, j=j: (j, 0, 0)),
                    const2((dq, 128)), const2((128, dq)), const2((dk, 128)), const2((128, dk)),
                    pl.BlockSpec((tb, dq), lambda b: (rope_block(b), 0)),
                    pl.BlockSpec((tb, dq), lambda b: (rope_block(b), 0)),
                ],
                out_specs=[
                    row_spec(dq),
                    pl.BlockSpec((N_KV_HEADS, tb, HEAD_DIM), lambda b: (0, b, 0)),
                    pl.BlockSpec((N_KV_HEADS, tb, HEAD_DIM), lambda b: (0, b, 0)),
                    row_spec(dk), row_spec(dk),
                ],
                out_shape=[
                    jax.ShapeDtypeStruct((tt, dq), BF16),
                    jax.ShapeDtypeStruct((N_KV_HEADS, tt, HEAD_DIM), BF16),
                    jax.ShapeDtypeStruct((N_KV_HEADS, tt, HEAD_DIM), BF16),
                    jax.ShapeDtypeStruct((tt, dk), F32),
                    jax.ShapeDtypeStruct((tt, dk), F32),
                ],
                compiler_params=_cparams("parallel"),
                name="qkv_project",
            )(x, mods, norm_mix3, w_qkv_bf, qg_all, kg_all, selq, selq_t, selk, selk_t, cos_t, sin_t)
            new_k.append(kf[:tp].reshape(batch, seq, N_KV_HEADS, HEAD_DIM))
            new_v.append(vf[:tp].reshape(batch, seq, N_KV_HEADS, HEAD_DIM))

            attn = _attention(q, k_hm, v_hm, ck_all, cv_all, j, npb, bps, dec_seq, past)

            x = pl.pallas_call(
                _proj_res_kernel,
                grid=(nblk,),
                in_specs=[row_spec(dq), pl.BlockSpec((None, dq, d), lambda b, j=j: (j, 0, 0)), row_spec(d), mod_spec(i)],
                out_specs=row_spec(d),
                out_shape=jax.ShapeDtypeStruct((tt, d), F32),
                compiler_params=_cparams("parallel"),
                name="attn_out_project",
            )(attn, w_o_bf, x, mods)
        else:
            u, gb = pl.pallas_call(
                _conv_in_kernel,
                grid=(nblk,),
                in_specs=[row_spec(d), mod_spec(i), layer_vec(i, d),
                          pl.BlockSpec((None, d, 3 * d), lambda b, j=j: (j, 0, 0))],
                out_specs=[row_spec(d), row_spec(d)],
                out_shape=[jax.ShapeDtypeStruct((tt, d), F32), jax.ShapeDtypeStruct((tt, d), BF16)],
                compiler_params=_cparams("parallel"),
                name="conv_in_project",
            )(x, mods, norm_mix3, w_bcx_bf)
            halo = tb // 8
            x = pl.pallas_call(
                functools.partial(_conv_out_kernel, n_prompt_blocks=npb, blocks_per_prompt_seq=bpp,
                                  blocks_per_sample_seq=bps),
                grid=(nblk,),
                in_specs=[
                    row_spec(d),
                    pl.BlockSpec((8, d), lambda b: (jnp.maximum(b * halo - 1, 0), 0)),
                    pl.BlockSpec((8, d), lambda b: (jnp.minimum((b + 1) * halo, nblk * halo - 1), 0)),
                    row_spec(d),
                    pl.BlockSpec((None, 3, d), lambda b, j=j: (j, 0, 0)),
                    pl.BlockSpec((None, 1, d), lambda b, j=j: (j, 0, 0)),
                    pl.BlockSpec((None, d, d), lambda b, j=j: (j, 0, 0)),
                    row_spec(d), mod_spec(i),
                ],
                out_specs=row_spec(d),
                out_shape=jax.ShapeDtypeStruct((tt, d), F32),
                compiler_params=_cparams("parallel"),
                name="conv_out_project",
            )(u, u, u, gb, conv_w, conv_b.reshape(-1, 1, d), w_co_bf, x, mods)

        h, tabs = _peer_route(x, mods, norm_ffn3, wf_all, i, cond_row)
        x = _peer_dense(h, u_bf, vt_bf, tabs, x, mods, i, cond_row, norm_final, final_norm=(i == depth - 1))

    y = x
    y_prompt = y[:tp].reshape(batch, seq, d)
    y_sample = y[tp:].reshape(dec_batch, dec_seq, d)
    return (y_prompt, y_sample, jnp.stack(new_k, axis=1), jnp.stack(new_v, axis=1))
```

```python
import functools
import math

import jax
import jax.numpy as jnp
from jax import lax
from jax.experimental import pallas as pl
from jax.experimental.pallas import tpu as pltpu

F32 = jnp.float32
BF16 = jnp.bfloat16

GRID_W = 64
N_HEADS = 16
N_KV_HEADS = 4
HEAD_DIM = 64
ROT_HALF = 16
ROPE_THETA = 10000.0
PEER_HEADS = 8
N_KEYS = 128
PEER_TOPK = 16
N_MOD = 6
EPS = 1e-6
N_COND_ROWS = 8
SUBLANES = 8

TOKEN_BLOCK = 256
DENSE_TOKENS = 512
DENSE_CHUNK = 2048
DENSE_X_ROWS = 256
DENSE_GROUP_BOUNDS = (0, 768, 1536, 1792, 2048)
DENSE_X_LEAD = 2
MASK_LANES = 256
ROUTE_LANES = 128
VMEM_LIMIT = 52 * 1024 * 1024


def _cparams(*sem):
    return pltpu.CompilerParams(dimension_semantics=sem, vmem_limit_bytes=VMEM_LIMIT)


def _split_bf16(a):
    hi = a.astype(BF16)
    lo = (a - hi.astype(F32)).astype(BF16)
    return hi, lo


def _norm_mod(x, nw, shift, scale):
    ms = jnp.mean(x * x, axis=-1, keepdims=True)
    return (x * lax.rsqrt(ms + EPS)) * nw * (1.0 + scale) + shift


def _gelu_tanh(x):
    c0 = math.sqrt(2.0 / math.pi)
    inner = x * (c0 + (c0 * 0.044715) * (x * x))
    return (0.5 * x) * (1.0 + jnp.tanh(inner))


def _dot_nt(a, b):
    return lax.dot_general(a, b, (((1,), (1,)), ((), ())), preferred_element_type=F32)


def _mod_kernel(c_ref, w_ref, b_ref, o_ref):
    c = c_ref[...]
    s_hi, s_lo = _split_bf16(jax.nn.silu(c))
    w_hi, w_lo = _split_bf16(w_ref[...])
    acc = jnp.dot(s_hi, w_hi, preferred_element_type=F32)
    acc += jnp.dot(s_hi, w_lo, preferred_element_type=F32)
    acc += jnp.dot(s_lo, w_hi, preferred_element_type=F32)
    o_ref[...] = acc + b_ref[...]


def _modulation(cond, w_ada, b_ada):
    depth, d, nd = w_ada.shape
    nb = 1536
    return pl.pallas_call(
        _mod_kernel,
        grid=(depth, nd // nb),
        in_specs=[
            pl.BlockSpec((N_COND_ROWS, d), lambda l, n: (0, 0)),
            pl.BlockSpec((None, d, nb), lambda l, n: (l, 0, n)),
            pl.BlockSpec((None, 1, nb), lambda l, n: (l, 0, n)),
        ],
        out_specs=pl.BlockSpec((None, N_COND_ROWS, nb), lambda l, n: (l, 0, n)),
        out_shape=jax.ShapeDtypeStruct((depth, N_COND_ROWS, nd), F32),
        compiler_params=_cparams("parallel", "parallel"),
        name="adaln_modulation",
    )(cond, w_ada, b_ada.reshape(depth, 1, nd))


def _fold_kernel(sk_ref, w_ref, o_ref):
    k_hi, k_lo = _split_bf16(sk_ref[...])
    w_hi, w_lo = _split_bf16(w_ref[...])
    acc = _dot_nt(k_hi, w_hi) + _dot_nt(k_hi, w_lo) + _dot_nt(k_lo, w_hi)
    o_ref[...] = acc.astype(BF16)


def _fold_keys(w_pq, sub_keys):
    depth, d, _ = w_pq.shape
    dkh = sub_keys.shape[-1]
    n_parts = PEER_HEADS * 2
    return pl.pallas_call(
        _fold_kernel,
        grid=(depth, n_parts),
        in_specs=[
            pl.BlockSpec((None, None, None, N_KEYS, dkh), lambda l, p: (l, p // 2, p % 2, 0, 0)),
            pl.BlockSpec((None, d, dkh), lambda l, p: (l, 0, p)),
        ],
        out_specs=pl.BlockSpec((None, N_KEYS, d), lambda l, p: (l, p, 0)),
        out_shape=jax.ShapeDtypeStruct((depth, n_parts * N_KEYS, d), BF16),
        compiler_params=_cparams("parallel", "parallel"),
        name="peer_fold_keys",
    )(sub_keys, w_pq)


def _head_rms_scale(q, sel, sel_t):
    hi, lo = _split_bf16(q * q)
    ss = jnp.dot(hi, sel, preferred_element_type=F32) + jnp.dot(lo, sel, preferred_element_type=F32)
    r = lax.rsqrt(ss * (1.0 / HEAD_DIM) + EPS)
    r_hi, r_lo = _split_bf16(r)
    return jnp.dot(r_hi, sel_t, preferred_element_type=F32) + jnp.dot(r_lo, sel_t, preferred_element_type=F32)


def _rope(x, cos, sin_signed):
    n = x.shape[1]
    lane = lax.broadcasted_iota(jnp.int32, x.shape, 1)
    upper = (lane & ROT_HALF) != 0
    partner = jnp.where(upper, pltpu.roll(x, ROT_HALF, axis=1), pltpu.roll(x, n - ROT_HALF, axis=1))
    return x * cos + partner * sin_signed


def _qkv_kernel(x_ref, mod_ref, nw_ref, w_ref, qg_ref, kg_ref, selq_ref, selqt_ref, selk_ref, selkt_ref,
                cos_ref, sin_ref, q_ref, k_ref, v_ref, kf_ref, vf_ref):
    dq = N_HEADS * HEAD_DIM
    dk = N_KV_HEADS * HEAD_DIM
    h = _norm_mod(x_ref[...], nw_ref[...], mod_ref[0:1, :], mod_ref[1:2, :]).astype(BF16)
    qkv = jnp.dot(h, w_ref[...], preferred_element_type=F32)
    q = qkv[:, :dq]
    k = qkv[:, dq:dq + dk]
    v = qkv[:, dq + dk:]
    cos = cos_ref[...]
    sin = sin_ref[...]
    qn = q * _head_rms_scale(q, selq_ref[...], selqt_ref[...]) * qg_ref[...]
    q_ref[...] = _rope(qn, cos, sin).astype(BF16)
    kn = k * _head_rms_scale(k, selk_ref[...], selkt_ref[...]) * kg_ref[...]
    kf_ref[...] = kn
    vf_ref[...] = v
    kr = _rope(kn, cos[:, :dk], sin[:, :dk]).astype(BF16)
    vb = v.astype(BF16)
    for g in range(N_KV_HEADS):
        k_ref[g] = kr[:, g * HEAD_DIM:(g + 1) * HEAD_DIM]
        v_ref[g] = vb[:, g * HEAD_DIM:(g + 1) * HEAD_DIM]


def _softmax_pv(qh, parts):
    scores = [_dot_nt(qh, k) for k, _ in parts]
    m = scores[0].max(axis=-1, keepdims=True)
    for s in scores[1:]:
        m = jnp.maximum(m, s.max(axis=-1, keepdims=True))
    num = None
    den = None
    for s, (_, v) in zip(scores, parts):
        p = jnp.exp2(s - m)
        d = p.sum(axis=-1, keepdims=True)
        o = jnp.dot(p.astype(BF16), v, preferred_element_type=F32)
        num = o if num is None else num + o
        den = d if den is None else den + d
    return num / den


def _attn_kernel(q_ref, ka_ref, va_ref, kb_ref, vb_ref, kc_ref, vc_ref, o_ref, s_ref, p_ref, l_ref, *,
                 n_prompt_blocks):
    u = pl.program_id(0)
    group = N_HEADS // N_KV_HEADS
    head = lambda hh: slice(hh * HEAD_DIM, (hh + 1) * HEAD_DIM)

    @pl.when(u < n_prompt_blocks)
    def _():
        parts = [(ka_ref[...], va_ref[...])]
        for hh in range(group):
            o_ref[:, head(hh)] = _softmax_pv(q_ref[:, head(hh)], parts).astype(BF16)

    @pl.when(u >= n_prompt_blocks)
    def _():
        lb = kb_ref.shape[0]

        def scores(hh):
            qh = q_ref[:, head(hh)]
            s_ref[hh, :, 0:lb] = _dot_nt(qh, kb_ref[...])
            s_ref[hh, :, lb:] = _dot_nt(qh, kc_ref[...])

        def softmax(hh):
            s = s_ref[hh]
            p = jnp.exp2(s - s.max(axis=-1, keepdims=True))
            l_ref[hh] = jnp.broadcast_to(p.sum(axis=-1, keepdims=True), l_ref.shape[1:])
            p_ref[hh] = p.astype(BF16)

        def weighted_values(hh):
            o = jnp.dot(p_ref[hh, :, 0:lb], vb_ref[...], preferred_element_type=F32)
            o = o + jnp.dot(p_ref[hh, :, lb:], vc_ref[...], preferred_element_type=F32)
            o_ref[:, head(hh)] = (o / l_ref[hh, :, 0:1]).astype(BF16)

        scores(0)
        for hh in range(group):
            if hh + 1 < group:
                scores(hh + 1)
            softmax(hh)
            if hh > 0:
                weighted_values(hh - 1)
        weighted_values(group - 1)


def _proj_res_kernel(a_ref, w_ref, x_ref, mod_ref, o_ref):
    o_ref[...] = x_ref[...] + mod_ref[2:3, :] * jnp.dot(a_ref[...], w_ref[...], preferred_element_type=F32)


def _conv_in_kernel(x_ref, mod_ref, nw_ref, w_ref, u_ref, gb_ref):
    d = x_ref.shape[1]
    h = _norm_mod(x_ref[...], nw_ref[...], mod_ref[0:1, :], mod_ref[1:2, :]).astype(BF16)
    bcx = jnp.dot(h, w_ref[...], preferred_element_type=F32)
    gb_ref[...] = bcx[:, :d].astype(BF16)
    u_ref[...] = bcx[:, d:2 * d] * bcx[:, 2 * d:]


def _conv_out_kernel(u_ref, up_ref, un_ref, gb_ref, cw_ref, cb_ref, w_ref, x_ref, mod_ref, o_ref, *,
                     n_prompt_blocks, blocks_per_prompt_seq, blocks_per_sample_seq):
    tb = u_ref.shape[0]
    b = pl.program_id(0)
    bs = b - n_prompt_blocks
    pos = jnp.where(b < n_prompt_blocks, b % blocks_per_prompt_seq, bs % blocks_per_sample_seq)
    per_seq = jnp.where(b < n_prompt_blocks, blocks_per_prompt_seq, blocks_per_sample_seq)
    u = u_ref[...]
    prev_row = jnp.where(pos == 0, 0.0, up_ref[7:8, :])
    next_row = jnp.where(pos == per_seq - 1, 0.0, un_ref[0:1, :])
    row = lax.broadcasted_iota(jnp.int32, u.shape, 0)
    u_m1 = jnp.where(row == 0, prev_row, pltpu.roll(u, 1, axis=0))
    u_p1 = jnp.where(row == tb - 1, next_row, pltpu.roll(u, tb - 1, axis=0))
    conv = u_m1 * cw_ref[0:1, :] + u * cw_ref[1:2, :] + u_p1 * cw_ref[2:3, :] + cb_ref[...]
    y = (gb_ref[...].astype(F32) * conv).astype(BF16)
    o_ref[...] = x_ref[...] + mod_ref[2:3, :] * jnp.dot(y, w_ref[...], preferred_element_type=F32)


def _oddeven_merge(lo, hi, r):
    step = r * 2
    if step < hi - lo:
        yield from _oddeven_merge(lo, hi, step)
        yield from _oddeven_merge(lo + r, hi, step)
        yield from [(i, i + r) for i in range(lo + r, hi - r, step)]
    else:
        yield (lo, lo + r)


def _oddeven_sort(lo, hi):
    if hi - lo >= 1:
        mid = lo + (hi - lo) // 2
        yield from _oddeven_sort(lo, mid)
        yield from _oddeven_sort(mid + 1, hi)
        yield from _oddeven_merge(lo, hi, 1)


def _bitonic_merge_pairs(n):
    out, s = [], n // 2
    while s >= 1:
        out += [(i, i + s) for i in range(n) if (i // s) % 2 == 0]
        s //= 2
    return out


SORT16 = tuple(_oddeven_sort(0, 15))
SORT8 = tuple(_oddeven_sort(0, 7))
BITONIC16 = tuple(_bitonic_merge_pairs(16))


def _apply_net(net, v):
    v = list(v)
    for i, j in net:
        hi, lo = jnp.maximum(v[i], v[j]), jnp.minimum(v[i], v[j])
        v[i], v[j] = hi, lo
    return v


def _merge_top16(a, b):
    return _apply_net(BITONIC16, [jnp.maximum(a[i], b[PEER_TOPK - 1 - i]) for i in range(PEER_TOPK)])


def _sorted_top16(x):
    v = _apply_net(SORT16, [x[SUBLANES * r:SUBLANES * (r + 1)] for r in range(N_KEYS // SUBLANES)])
    for shift in (4, 2, 1):
        v = _merge_top16(v, [pltpu.roll(u, shift, axis=0) for u in v])
    return v


def _pair_threshold(a, b):
    shape = a[0].shape
    neg = jnp.full(shape, -jnp.inf, F32)
    c = {(p, q): a[p - 1] + b[q - 1] for p in range(1, PEER_TOPK + 1) for q in range(1, PEER_TOPK // p + 1)}
    row1 = [c[(1, q)] for q in range(1, 17)]
    col1 = [c[(p, 1)] for p in range(2, 17)] + [neg]
    row2 = [c[(2, q)] for q in range(2, 9)] + [neg]
    col2 = [c[(p, 2)] for p in range(3, 9)] + [neg, neg]
    rest = [c[(3, 3)], c[(3, 4)], c[(3, 5)], c[(4, 3)], c[(5, 3)], c[(4, 4)], neg, neg]
    m2 = _apply_net(BITONIC16, row2 + col2[::-1])
    m3 = _apply_net(SORT8, rest) + [neg] * 8
    top = _merge_top16(_merge_top16(row1, col1), _merge_top16(m2, m3))
    tau = top[PEER_TOPK - 1]
    z = jnp.zeros(shape, F32)
    for v in top:
        z = z + jnp.exp(v - top[0])
    n_of_rank = []
    for p in range(1, PEER_TOPK + 1):
        cnt = jnp.zeros(shape, F32)
        for q in range(1, PEER_TOPK // p + 1):
            cnt = cnt + jnp.where(c[(p, q)] >= tau, 1.0, 0.0)
        n_of_rank.append(cnt)
    return n_of_rank, 1.0 / z


def _key_tables(s1, s2, a, b, n_of_rank, inv_z):
    t = s1.shape[1]
    n_rows, e1_rows, r2_rows, e2_rows = [], [], [], []
    for r in range(N_KEYS // SUBLANES):
        x1 = s1[SUBLANES * r:SUBLANES * (r + 1)]
        x2 = s2[SUBLANES * r:SUBLANES * (r + 1)]
        n = jnp.zeros((SUBLANES, t), F32)
        r2 = jnp.full((SUBLANES, t), float(PEER_TOPK), F32)
        for q in range(PEER_TOPK - 1, -1, -1):
            n = jnp.where(x1 >= a[q], n_of_rank[q], n)
            r2 = jnp.where(x2 >= b[q], float(q), r2)
        n_rows.append(n)
        r2_rows.append(r2)
        e1_rows.append(jnp.exp(x1 - a[0]) * inv_z)
        e2_rows.append(jnp.exp(x2 - b[0]))
    cat = lambda rows: jnp.concatenate(rows, axis=0)
    return cat(n_rows), cat(e1_rows), cat(r2_rows), cat(e2_rows)


def _row_to_all_rows(x, row):
    sub = lax.broadcasted_iota(jnp.int32, x.shape, 0)
    y = jnp.where(sub == row, x, 0.0)
    for shift in (4, 2, 1):
        y = y + pltpu.roll(y, shift, axis=0)
    return y


def _route_kernel(x_ref, mod_ref, nw_ref, wf_ref, h_ref, n_ref, e1_ref, r2_ref, e2_ref,
                  s_ref, lists_ref, heads_ref, stats_ref):
    h = _norm_mod(x_ref[...], nw_ref[...], mod_ref[3:4, :], mod_ref[4:5, :]).astype(BF16)
    h_ref[...] = h
    t = s_ref.shape[1]
    hk = 2 * N_KEYS

    def score_head(hd):
        s_ref[hd * hk:(hd + 1) * hk, :] = _dot_nt(wf_ref[hd * hk:(hd + 1) * hk, :], h)
    k2 = 2 * PEER_TOPK
    lane_groups = [slice(l * ROUTE_LANES, (l + 1) * ROUTE_LANES) for l in range(t // ROUTE_LANES)]
    heads_ref[...] = jnp.zeros_like(heads_ref)

    def half_scores(hd, ls):
        r1 = pl.multiple_of(hd * 2 * N_KEYS, 2 * N_KEYS)
        return s_ref[pl.ds(r1, N_KEYS), ls], s_ref[pl.ds(r1 + N_KEYS, N_KEYS), ls]

    def sort_head(hd, carry):
        for ls in lane_groups:
            s1, s2 = half_scores(hd, ls)
            ab = _sorted_top16(s1) + _sorted_top16(s2)
            sub = lax.broadcasted_iota(jnp.int32, ab[0].shape, 0)
            for q in range(k2):
                lists_ref[hd, q, :, ls] = ab[q]
                heads_ref[q, :, ls] = jnp.where(sub == hd, ab[q], heads_ref[q, :, ls])
        return carry

    score_head(0)
    score_head(1)
    for hd in range(PEER_HEADS):
        if hd + 2 < PEER_HEADS:
            score_head(hd + 2)
        sort_head(hd, 0)

    for ls in lane_groups:
        ab = [heads_ref[q, :, ls] for q in range(k2)]
        n_of_rank, inv_z = _pair_threshold(ab[:PEER_TOPK], ab[PEER_TOPK:])
        for p in range(PEER_TOPK):
            stats_ref[p, :, ls] = n_of_rank[p]
        stats_ref[PEER_TOPK, :, ls] = inv_z

    def tables_head(hd, carry):
        for ls in lane_groups:
            s1, s2 = half_scores(hd, ls)
            ab = [lists_ref[hd, q, :, ls] for q in range(k2)]
            stats = [_row_to_all_rows(stats_ref[p, :, ls], hd) for p in range(PEER_TOPK + 1)]
            n, e1, r2, e2 = _key_tables(s1, s2, ab[:PEER_TOPK], ab[PEER_TOPK:], stats[:PEER_TOPK], stats[PEER_TOPK])
            n_ref[hd, :, ls] = n
            e1_ref[hd, :, ls] = e1
            r2_ref[hd, :, ls] = r2.astype(BF16)
            e2_ref[hd, :, ls] = e2.astype(BF16)
        return carry

    lax.fori_loop(0, PEER_HEADS, tables_head, 0)


def _dense_kernel(h_ref, u_ref, vt_ref, n_ref, e1_ref, r2_ref, e2_ref, x_ref, mod_ref, nf_ref, o_ref,
                  acc_ref, a_ref, p_ref, *, final_norm):
    c = pl.program_id(1)
    tb = h_ref.shape[0]
    d = vt_ref.shape[0]
    sub = SUBLANES
    xr = DENSE_X_ROWS
    bounds = DENSE_GROUP_BOUNDS
    n_groups = len(bounds) - 1

    @pl.when(c == 0)
    def _():
        acc_ref[...] = jnp.zeros_like(acc_ref)

    h = h_ref[...]

    def x_group(g):
        for r0 in range(bounds[g], bounds[g + 1], xr):
            js = slice(r0, r0 + xr)
            a_ref[js, :] = _dot_nt(u_ref[js, :], h)

    def mask_group(g):
        for ii in range(bounds[g] // N_KEYS, bounds[g + 1] // N_KEYS):
            rs = slice(ii * N_KEYS, (ii + 1) * N_KEYS)
            for l in range(tb // MASK_LANES):
                ls = slice(l * MASK_LANES, (l + 1) * MASK_LANES)
                w = jnp.zeros((N_KEYS // sub, sub, MASK_LANES), BF16)
                for hd in range(PEER_HEADS):
                    nb = jnp.broadcast_to(n_ref[hd, ii:ii + 1, ls], (sub, MASK_LANES)).astype(BF16)
                    eb = jnp.broadcast_to(e1_ref[hd, ii:ii + 1, ls], (sub, MASK_LANES)).astype(BF16)
                    r2 = r2_ref[hd, :, ls].reshape(N_KEYS // sub, sub, MASK_LANES)
                    e2 = e2_ref[hd, :, ls].reshape(N_KEYS // sub, sub, MASK_LANES)
                    w = w + jnp.where(r2 < nb[None], e2 * eb[None], jnp.zeros_like(e2))
                p_ref[rs, ls] = _gelu_tanh(a_ref[rs, ls].astype(BF16)) * w.reshape(N_KEYS, MASK_LANES)

    def z_group(g):
        ks = slice(bounds[g], bounds[g + 1])
        for m in range(d // xr):
            ms = slice(m * xr, (m + 1) * xr)
            acc_ref[ms, :] += jnp.dot(vt_ref[ms, ks], p_ref[ks, :], preferred_element_type=F32)

    for g in range(min(DENSE_X_LEAD, n_groups)):
        x_group(g)
    for g in range(n_groups):
        if g + DENSE_X_LEAD < n_groups:
            x_group(g + DENSE_X_LEAD)
        mask_group(g)
        if g > 0:
            z_group(g - 1)
    z_group(n_groups - 1)

    @pl.when(c == pl.num_programs(1) - 1)
    def _():
        y = x_ref[...] + mod_ref[5:6, :] * acc_ref[...].T
        if final_norm:
            ms = jnp.mean(y * y, axis=-1, keepdims=True)
            y = y * lax.rsqrt(ms + EPS) * nf_ref[...]
        o_ref[...] = y


def _rope_tables(seq_len, n_identity):
    axis_dim = HEAD_DIM // 2
    n_rows = seq_len // GRID_W
    row = jnp.repeat(jnp.arange(n_rows, dtype=F32), GRID_W)
    col = jnp.tile(jnp.arange(GRID_W, dtype=F32), n_rows)
    inv = ROPE_THETA ** (-jnp.arange(0, axis_dim, 2, dtype=F32) / axis_dim)
    ang_r = row[:, None] * inv
    ang_c = col[:, None] * inv
    cos = jnp.concatenate([jnp.cos(ang_r)] * 2 + [jnp.cos(ang_c)] * 2, axis=-1)
    sin = jnp.concatenate([-jnp.sin(ang_r), jnp.sin(ang_r), -jnp.sin(ang_c), jnp.sin(ang_c)], axis=-1)
    cos = jnp.concatenate([jnp.ones((n_identity, HEAD_DIM), F32), cos], axis=0)
    sin = jnp.concatenate([jnp.zeros((n_identity, HEAD_DIM), F32), sin], axis=0)
    return jnp.tile(cos, (1, N_HEADS)), jnp.tile(sin, (1, N_HEADS))


def _head_selectors(n_heads):
    c = jnp.arange(n_heads * HEAD_DIM)[:, None] // HEAD_DIM
    sel = (c == jnp.arange(128)[None, :]).astype(BF16)
    return sel, sel.T


def _attention(q, k_hm, v_hm, ck_all, cv_all, layer_j, n_prompt_blocks, blocks_per_sample_seq, dec_seq, past):
    tt, dq = q.shape
    tb = TOKEN_BLOCK
    dk = N_KV_HEADS * HEAD_DIM
    npb, bps = n_prompt_blocks, blocks_per_sample_seq
    tp = npb * tb
    group = N_HEADS // N_KV_HEADS

    def sample_seq(u):
        return jnp.maximum(u - npb, 0) // bps

    kv_a = pl.BlockSpec((None, tb, HEAD_DIM), lambda u, g: (g, jnp.minimum(u, npb - 1), 0))
    kv_b = pl.BlockSpec((None, dec_seq, HEAD_DIM), lambda u, g: (g, tp // dec_seq + sample_seq(u), 0))
    kv_c = pl.BlockSpec((None, None, None, past, HEAD_DIM), lambda u, g: (layer_j, sample_seq(u), g, 0, 0))
    return pl.pallas_call(
        functools.partial(_attn_kernel, n_prompt_blocks=npb),
        grid=(tt // tb, N_KV_HEADS),
        in_specs=[pl.BlockSpec((tb, dk), lambda u, g: (u, g)), kv_a, kv_a, kv_b, kv_b, kv_c, kv_c],
        out_specs=pl.BlockSpec((tb, dk), lambda u, g: (u, g)),
        out_shape=jax.ShapeDtypeStruct((tt, dq), BF16),
        scratch_shapes=[pltpu.VMEM((group, tb, dec_seq + past), F32), pltpu.VMEM((group, tb, dec_seq + past), BF16),
                        pltpu.VMEM((group, tb, 128), F32)],
        compiler_params=_cparams("parallel", "parallel"),
        name="attention",
    )(q, k_hm, v_hm, k_hm, v_hm, ck_all, cv_all)


def _peer_route(x, mods, norm_ffn3, wf_all, layer, cond_row):
    tt, d = x.shape
    tb = TOKEN_BLOCK
    n_keys2 = PEER_HEADS * 2 * N_KEYS
    tab = lambda dtype: jax.ShapeDtypeStruct((PEER_HEADS, N_KEYS, tt), dtype)
    tab_spec = pl.BlockSpec((PEER_HEADS, N_KEYS, tb), lambda b: (0, 0, b))
    row_spec = pl.BlockSpec((tb, d), lambda b: (b, 0))
    h, *tabs = pl.pallas_call(
        _route_kernel,
        grid=(tt // tb,),
        in_specs=[row_spec,
                  pl.BlockSpec((None, None, N_MOD, d), lambda b: (layer, cond_row(b, tb), 0, 0)),
                  pl.BlockSpec((None, 1, d), lambda b: (layer, 0, 0)),
                  pl.BlockSpec((None, n_keys2, d), lambda b: (layer, 0, 0))],
        out_specs=[row_spec, tab_spec, tab_spec, tab_spec, tab_spec],
        out_shape=[jax.ShapeDtypeStruct((tt, d), BF16), tab(F32), tab(F32), tab(BF16), tab(BF16)],
        scratch_shapes=[pltpu.VMEM((n_keys2, tb), F32),
                        pltpu.VMEM((PEER_HEADS, 2 * PEER_TOPK, SUBLANES, tb), F32),
                        pltpu.VMEM((2 * PEER_TOPK, SUBLANES, tb), F32),
                        pltpu.VMEM((PEER_TOPK + 1, SUBLANES, tb), F32)],
        compiler_params=_cparams("parallel"),
        name="peer_route",
    )(x, mods, norm_ffn3, wf_all)
    return h, tabs


def _peer_dense(h, u_bf, vt_bf, tabs, x, mods, layer, cond_row, norm_final, final_norm):
    tt, d = x.shape
    n_exp = u_bf.shape[1]
    dt, de = DENSE_TOKENS, DENSE_CHUNK
    n_tab, e1_tab, r2_tab, e2_tab = tabs
    dtab = pl.BlockSpec((PEER_HEADS, N_KEYS, dt), lambda b, e: (0, 0, b))
    rtab = pl.BlockSpec((PEER_HEADS, de // N_KEYS, dt), lambda b, e: (0, e, b))
    return pl.pallas_call(
        functools.partial(_dense_kernel, final_norm=final_norm),
        grid=(tt // dt, n_exp // de),
        in_specs=[
            pl.BlockSpec((dt, d), lambda b, e: (b, 0)),
            pl.BlockSpec((None, de, d), lambda b, e: (layer, e, 0)),
            pl.BlockSpec((None, d, de), lambda b, e: (layer, 0, e)),
            rtab, rtab, dtab, dtab,
            pl.BlockSpec((dt, d), lambda b, e: (b, 0)),
            pl.BlockSpec((None, None, N_MOD, d), lambda b, e: (layer, cond_row(b, dt), 0, 0)),
            pl.BlockSpec((1, d), lambda b, e: (0, 0)),
        ],
        out_specs=pl.BlockSpec((dt, d), lambda b, e: (b, 0)),
        out_shape=jax.ShapeDtypeStruct((tt, d), F32),
        scratch_shapes=[pltpu.VMEM((d, dt), F32), pltpu.VMEM((de, dt), F32), pltpu.VMEM((de, dt), BF16)],
        compiler_params=_cparams("parallel", "arbitrary"),
        name="peer_dense",
    )(h, u_bf, vt_bf, n_tab, e1_tab, r2_tab, e2_tab, x, mods, norm_final.reshape(1, d))


def kernel(x_prompt, x_sample, cache_k, cache_v, c, c_ctx, w_ada, b_ada, norm_mix, norm_ffn, norm_final,
           w_qkv, q_gain, k_gain, w_o, w_bcx, conv_w, conv_b, w_conv_out, w_pq, sub_keys, u_exp, v_exp):
    batch, seq, d = x_prompt.shape
    dec_batch, dec_seq, _ = x_sample.shape
    depth = w_ada.shape[0]
    past = cache_k.shape[2]
    tp = batch * seq
    ts = dec_batch * dec_seq
    tt = tp + ts
    tb = TOKEN_BLOCK
    assert seq % tb == 0 and dec_seq % tb == 0 and dec_batch + 1 <= N_COND_ROWS
    assert tp % DENSE_TOKENS == 0 and dec_seq % DENSE_TOKENS == 0
    assert d == N_HEADS * HEAD_DIM and seq == tb and past == tb and tp % dec_seq == 0
    npb = tp // tb
    bpp = seq // tb
    bps = dec_seq // tb
    nblk = tt // tb
    dq = N_HEADS * HEAD_DIM
    dk = N_KV_HEADS * HEAD_DIM
    n_exp = u_exp.shape[1]
    assert n_exp == N_KEYS * N_KEYS and n_exp % DENSE_CHUNK == 0
    assert DENSE_GROUP_BOUNDS[0] == 0 and DENSE_GROUP_BOUNDS[-1] == DENSE_CHUNK
    assert all(b % DENSE_X_ROWS == 0 for b in DENSE_GROUP_BOUNDS) and d % DENSE_X_ROWS == 0

    def cond_row(b, rows=tb):
        t0 = b * rows
        return jnp.where(t0 < tp, 0, 1 + (t0 - tp) // dec_seq)

    def mod_spec(layer):
        return pl.BlockSpec((None, None, N_MOD, d), lambda b: (layer, cond_row(b), 0, 0))

    def row_spec(width, dtype_rows=tb):
        return pl.BlockSpec((dtype_rows, width), lambda b: (b, 0))

    def layer_vec(layer, width):
        return pl.BlockSpec((None, 1, width), lambda b: (layer, 0, 0))

    def const2(shape):
        return pl.BlockSpec(shape, lambda b: (0, 0))

    x = jnp.concatenate([x_prompt.reshape(tp, d), x_sample.reshape(ts, d)], axis=0)
    cond = jnp.zeros((N_COND_ROWS, d), F32).at[0].set(c_ctx).at[1:1 + dec_batch].set(c)
    mods = _modulation(cond, w_ada, b_ada).reshape(depth, N_COND_ROWS, N_MOD, d)

    wf_all = _fold_keys(w_pq, sub_keys)
    u_bf = u_exp.astype(BF16)
    vt_bf = jnp.transpose(v_exp, (0, 2, 1)).astype(BF16)
    w_qkv_bf = w_qkv.astype(BF16)
    w_o_bf = w_o.astype(BF16)
    w_bcx_bf = w_bcx.astype(BF16)
    w_co_bf = w_conv_out.astype(BF16)
    norm_mix3 = norm_mix.reshape(depth, 1, d)
    norm_ffn3 = norm_ffn.reshape(depth, 1, d)

    cos_t, sin_t = _rope_tables(dec_seq, tb)
    selq, selq_t = _head_selectors(N_HEADS)
    selk, selk_t = _head_selectors(N_KV_HEADS)
    scale = HEAD_DIM ** -0.5 * math.log2(math.e)
    qg_all = jnp.tile(q_gain, (1, N_HEADS)).reshape(-1, 1, dq) * scale
    kg_all = jnp.tile(k_gain, (1, N_KV_HEADS)).reshape(-1, 1, dk)
    ck_all = jnp.transpose(cache_k, (1, 0, 3, 2, 4)).astype(BF16)
    cv_all = jnp.transpose(cache_v, (1, 0, 3, 2, 4)).astype(BF16)

    def rope_block(b):
        return jnp.where(b < npb, 0, 1 + (b - npb) % bps)

    new_k, new_v = [], []
    for i in range(depth):
        j = i // 2
        if i % 2 == 0:
            q, k_hm, v_hm, kf, vf = pl.pallas_call(
                _qkv_kernel,
                grid=(nblk,),
                in_specs=[
                    row_spec(d), mod_spec(i), layer_vec(i, d),
                    pl.BlockSpec((None, d, dq + 2 * dk), lambda b, j=j: (j, 0, 0)),
                    pl.BlockSpec((None, 1, dq), lambda b, j=j: (j, 0, 0)),
                    pl.BlockSpec((None, 1, dk), lambda b, j=j: (j, 0, 0)),
                    const2((dq, 128)), const2((128, dq)), const2((dk, 128)), const2((128, dk)),
                    pl.BlockSpec((tb, dq), lambda b: (rope_block(b), 0)),
                    pl.BlockSpec((tb, dq), lambda b: (rope_block(b), 0)),
                ],
                out_specs=[
                    row_spec(dq),
                    pl.BlockSpec((N_KV_HEADS, tb, HEAD_DIM), lambda b: (0, b, 0)),
                    pl.BlockSpec((N_KV_HEADS, tb, HEAD_DIM), lambda b: (0, b, 0)),
                    row_spec(dk), row_spec(dk),
                ],
                out_shape=[
                    jax.ShapeDtypeStruct((tt, dq), BF16),
                    jax.ShapeDtypeStruct((N_KV_HEADS, tt, HEAD_DIM), BF16),
                    jax.ShapeDtypeStruct((N_KV_HEADS, tt, HEAD_DIM), BF16),
                    jax.ShapeDtypeStruct((tt, dk), F32),
                    jax.ShapeDtypeStruct((tt, dk), F32),
                ],
                compiler_params=_cparams("parallel"),
                name="qkv_project",
            )(x, mods, norm_mix3, w_qkv_bf, qg_all, kg_all, selq, selq_t, selk, selk_t, cos_t, sin_t)
            new_k.append(kf[:tp].reshape(batch, seq, N_KV_HEADS, HEAD_DIM))
            new_v.append(vf[:tp].reshape(batch, seq, N_KV_HEADS, HEAD_DIM))

            attn = _attention(q, k_hm, v_hm, ck_all, cv_all, j, npb, bps, dec_seq, past)

            x = pl.pallas_call(
                _proj_res_kernel,
                grid=(nblk,),
                in_specs=[row_spec(dq), pl.BlockSpec((None, dq, d), lambda b, j=j: (j, 0, 0)), row_spec(d), mod_spec(i)],
                out_specs=row_spec(d),
                out_shape=jax.ShapeDtypeStruct((tt, d), F32),
                compiler_params=_cparams("parallel"),
                name="attn_out_project",
            )(attn, w_o_bf, x, mods)
        else:
            u, gb = pl.pallas_call(
                _conv_in_kernel,
                grid=(nblk,),
                in_specs=[row_spec(d), mod_spec(i), layer_vec(i, d),
                          pl.BlockSpec((None, d, 3 * d), lambda b, j=j: (j, 0, 0))],
                out_specs=[row_spec(d), row_spec(d)],
                out_shape=[jax.ShapeDtypeStruct((tt, d), F32), jax.ShapeDtypeStruct((tt, d), BF16)],
                compiler_params=_cparams("parallel"),
                name="conv_in_project",
            )(x, mods, norm_mix3, w_bcx_bf)
            halo = tb // 8
            x = pl.pallas_call(
                functools.partial(_conv_out_kernel, n_prompt_blocks=npb, blocks_per_prompt_seq=bpp,
                                  blocks_per_sample_seq=bps),
                grid=(nblk,),
                in_specs=[
                    row_spec(d),
                    pl.BlockSpec((8, d), lambda b: (jnp.maximum(b * halo - 1, 0), 0)),
                    pl.BlockSpec((8, d), lambda b: (jnp.minimum((b + 1) * halo, nblk * halo - 1), 0)),
                    row_spec(d),
                    pl.BlockSpec((None, 3, d), lambda b, j=j: (j, 0, 0)),
                    pl.BlockSpec((None, 1, d), lambda b, j=j: (j, 0, 0)),
                    pl.BlockSpec((None, d, d), lambda b, j=j: (j, 0, 0)),
                    row_spec(d), mod_spec(i),
                ],
                out_specs=row_spec(d),
                out_shape=jax.ShapeDtypeStruct((tt, d), F32),
                compiler_params=_cparams("parallel"),
                name="conv_out_project",
            )(u, u, u, gb, conv_w, conv_b.reshape(-1, 1, d), w_co_bf, x, mods)

        h, tabs = _peer_route(x, mods, norm_ffn3, wf_all, i, cond_row)
        x = _peer_dense(h, u_bf, vt_bf, tabs, x, mods, i, cond_row, norm_final, final_norm=(i == depth - 1))

    y = x
    y_prompt = y[:tp].reshape(batch, seq, d)
    y_sample = y[tp:].reshape(dec_batch, dec_seq, d)
    return (y_prompt, y_sample, jnp.stack(new_k, axis=1), jnp.stack(new_v, axis=1))
```
